```python
import jax, jax.numpy as jnp
from jax import lax
import numpy as np

D_MODEL = 2048
BATCH = 8
SEQ = 4096
DEPTH = 1
DEC_BATCH = 16
DEC_SEQ = 2048
PAST_LEN = 128

N_MEM = 256
W_BR = D_MODEL
N_BRANCH = 3
NH_M = 8
HD_M = W_BR // NH_M
CHUNK = 128
CONV_W = 3
NH_A = 4
HD_A = W_BR // NH_A
N_IN = 5 * W_BR + 4 * NH_M + 4 * W_BR + 2 * W_BR + N_BRANCH * D_MODEL
EPS = 1e-6

kernel_name = 'bidir_mlstm_shortconv_memattn_hybrid'


def rmsnorm(x, g):
    xf = x.astype(jnp.float32)
    y = xf * lax.rsqrt(jnp.mean(xf * xf, axis=-1, keepdims=True) + EPS) * g.astype(jnp.float32)
    return y.astype(x.dtype)


def split_in(p):
    sizes = [W_BR] * 5 + [4 * NH_M] + [W_BR] * 4 + [W_BR] * 2 + [N_BRANCH * D_MODEL]
    idx = np.cumsum(sizes)[:-1].tolist()
    return jnp.split(p, idx, axis=-1)


def mlstm_one_direction(q, k, v, i_pre, f_pre):
    B, H, S, d = q.shape
    nc = S // CHUNK

    def to_chunks(t):
        return jnp.moveaxis(t.reshape((B, H, nc, CHUNK) + t.shape[3:]), 2, 0)

    xs = (to_chunks(q), to_chunks(k), to_chunks(v), to_chunks(i_pre), to_chunks(jax.nn.log_sigmoid(f_pre)))
    tril = jnp.tril(jnp.ones((CHUNK, CHUNK), dtype=bool))

    def body(carry, xc):
        C, n, m = carry
        qc, kc, vc, ic, lfc = xc
        b = jnp.cumsum(lfc, axis=-1)
        dmat = b[..., :, None] - b[..., None, :] + ic[..., None, :]
        dmat = jnp.where(tril, dmat, -jnp.inf)
        inter = b + m[..., None]
        m_t = jnp.maximum(jnp.max(dmat, axis=-1), inter)
        s = jnp.einsum('bhtd,bhsd->bhts', qc, kc) * jnp.exp(dmat - m_t[..., None])
        a = jnp.exp(inter - m_t)
        num = jnp.einsum('bhts,bhsd->bhtd', s, vc) + a[..., None] * jnp.einsum('bhtd,bhde->bhte', qc, C)
        den = jnp.sum(s, axis=-1) + a * jnp.einsum('bhtd,bhd->bht', qc, n)
        h = num / jnp.maximum(jnp.abs(den), jnp.exp(-m_t))[..., None]
        bl = b[..., -1]
        wlog = bl[..., None] - b + ic
        m_new = jnp.maximum(bl + m, jnp.max(wlog, axis=-1))
        w = jnp.exp(wlog - m_new[..., None])
        decay = jnp.exp(bl + m - m_new)
        kw = kc * w[..., None]
        C_new = decay[..., None, None] * C + jnp.einsum('bhsd,bhse->bhde', kw, vc)
        n_new = decay[..., None] * n + jnp.sum(kw, axis=2)
        return (C_new, n_new, m_new), h

    init = (jnp.zeros((B, H, d, d), jnp.float32), jnp.zeros((B, H, d), jnp.float32),
            jnp.zeros((B, H), jnp.float32))
    _, hs = lax.scan(body, init, xs)
    return jnp.moveaxis(hs, 0, 2).reshape(B, H, S, d)


def mlstm_branch(q_m, k_m, v_m, o_m, z_m, gates, b_if, mh_g):
    B, S, _ = q_m.shape
    f32 = jnp.float32

    def heads(t):
        return t.reshape(B, S, NH_M, HD_M).transpose(0, 2, 1, 3).astype(f32)

    q = heads(q_m)
    k = heads(k_m) * (HD_M ** -0.5)
    v = heads(v_m)
    g = (gates.astype(f32) + b_if.astype(f32)).transpose(0, 2, 1)
    i_f, i_b, f_f, f_b = jnp.split(g, 4, axis=1)
    h_fwd = mlstm_one_direction(q, k, v, i_f, f_f)
    fl = lambda t: jnp.flip(t, axis=2)
    h_bwd = fl(mlstm_one_direction(fl(q), fl(k), fl(v), fl(i_b), fl(f_b)))
    h = h_fwd + h_bwd
    mu = jnp.mean(h, axis=-1, keepdims=True)
    var = jnp.mean((h - mu) ** 2, axis=-1, keepdims=True)
    h = (h - mu) * lax.rsqrt(var + EPS)
    h = h.transpose(0, 2, 1, 3).reshape(B, S, W_BR) * mh_g.astype(f32)
    h = h.astype(q_m.dtype)
    return h * jax.nn.sigmoid(o_m) * jax.nn.silu(z_m)


def shortconv_branch(cb, cc, cx, z_c, conv_w):
    u = cc * cx
    up = jnp.pad(u, ((0, 0), (1, 1), (0, 0)))
    y = conv_w[0] * up[:, :-2] + conv_w[1] * up[:, 1:-1] + conv_w[2] * up[:, 2:]
    return cb * y * jax.nn.silu(z_c)


def memattn_branch(q_a, z_a, mem, mem_g, w_kv):
    B, S, _ = q_a.shape
    kv = rmsnorm(mem, mem_g) @ w_kv
    km, vm = jnp.split(kv, 2, axis=-1)
    q = q_a.reshape(B, S, NH_A, HD_A)
    km = km.reshape(B, N_MEM, NH_A, HD_A)
    vm = vm.reshape(B, N_MEM, NH_A, HD_A)
    s = jnp.einsum('bshd,bmhd->bhsm', q, km).astype(jnp.float32) * (HD_A ** -0.5)
    p = jax.nn.softmax(s, axis=-1).astype(vm.dtype)
    o = jnp.einsum('bhsm,bmhd->bshd', p, vm).reshape(B, S, W_BR)
    return o * jax.nn.silu(z_a)


def hybrid_layer(x, mem, norm_g, w_in, b_if, conv_w, mem_norm_g, w_kv_mem, mh_norm_g, w_branch, w_out):
    B, S, _ = x.shape
    h = rmsnorm(x, norm_g)
    (q_m, k_m, v_m, o_m, z_m, gates, cb, cc, cx, z_c, q_a, z_a, mg) = split_in(h @ w_in)
    y_m = mlstm_branch(q_m, k_m, v_m, o_m, z_m, gates, b_if, mh_norm_g)
    y_c = shortconv_branch(cb, cc, cx, z_c, conv_w)
    y_a = memattn_branch(q_a, z_a, mem, mem_norm_g, w_kv_mem)
    mg = jax.nn.sigmoid(mg.reshape(B, S, N_BRANCH, D_MODEL))
    merged = (mg[:, :, 0] * (y_m @ w_branch[0])
              + mg[:, :, 1] * (y_c @ w_branch[1])
              + mg[:, :, 2] * (y_a @ w_branch[2]))
    return x + merged @ w_out


def setup_inputs(seed: int = 0) -> dict:
    key = jax.random.key(seed)
    ks = jax.random.split(key, 16)
    nrm = jax.random.normal
    x_prompt = nrm(ks[0], (BATCH, SEQ, D_MODEL), jnp.float32)
    x_sample = nrm(ks[1], (DEC_BATCH, DEC_SEQ, D_MODEL), jnp.float32)
    mem_prompt = nrm(ks[2], (BATCH, N_MEM, D_MODEL), jnp.float32)
    mem_sample = nrm(ks[3], (DEC_BATCH, N_MEM, D_MODEL), jnp.float32)
    norm_g = 1.0 + 0.02 * nrm(ks[4], (DEPTH, D_MODEL), jnp.float32)
    w_in = nrm(ks[5], (DEPTH, D_MODEL, N_IN), jnp.float32) * (D_MODEL ** -0.5)
    b_i = 0.1 * nrm(ks[6], (DEPTH, 2 * NH_M), jnp.float32)
    b_f = jnp.tile(jnp.linspace(3.0, 6.0, NH_M, dtype=jnp.float32), 2)[None] + 0.1 * nrm(ks[7], (DEPTH, 2 * NH_M), jnp.float32)
    b_if = jnp.concatenate([b_i, b_f], axis=-1)
    conv_w = nrm(ks[8], (DEPTH, CONV_W, W_BR), jnp.float32) * (CONV_W ** -0.5)
    mem_norm_g = 1.0 + 0.02 * nrm(ks[9], (DEPTH, D_MODEL), jnp.float32)
    w_kv_mem = nrm(ks[10], (DEPTH, D_MODEL, 2 * W_BR), jnp.float32) * (D_MODEL ** -0.5)
    mh_norm_g = 1.0 + 0.02 * nrm(ks[11], (DEPTH, W_BR), jnp.float32)
    w_branch = nrm(ks[12], (DEPTH, N_BRANCH, W_BR, D_MODEL), jnp.float32) * (W_BR ** -0.5)
    w_out = nrm(ks[13], (DEPTH, D_MODEL, D_MODEL), jnp.float32) * (D_MODEL ** -0.5)
    final_norm_g = 1.0 + 0.02 * nrm(ks[14], (D_MODEL,), jnp.float32)
    return {'x_prompt': x_prompt, 'x_sample': x_sample, 'mem_prompt': mem_prompt, 'mem_sample': mem_sample,
            'norm_g': norm_g, 'w_in': w_in, 'b_if': b_if, 'conv_w': conv_w, 'mem_norm_g': mem_norm_g,
            'w_kv_mem': w_kv_mem, 'mh_norm_g': mh_norm_g, 'w_branch': w_branch, 'w_out': w_out,
            'final_norm_g': final_norm_g}


def reference(x_prompt, x_sample, mem_prompt, mem_sample, norm_g, w_in, b_if, conv_w, mem_norm_g,
              w_kv_mem, mh_norm_g, w_branch, w_out, final_norm_g):
    def trunk(x, mem):
        for l in range(DEPTH):
            x = hybrid_layer(x, mem, norm_g[l], w_in[l], b_if[l], conv_w[l], mem_norm_g[l], w_kv_mem[l],
                             mh_norm_g[l], w_branch[l], w_out[l])
        return rmsnorm(x, final_norm_g)

    y_prompt = trunk(x_prompt, mem_prompt)
    y_sample = trunk(x_sample, mem_sample)
    return (y_prompt, y_sample)
```

```python
import functools

import jax
import jax.numpy as jnp
from jax import lax
from jax.experimental import pallas as pl
from jax.experimental.pallas import tpu as pltpu

D_MODEL = 2048
NH_M = 8
HD_M = D_MODEL // NH_M
CHUNK = 128
NH_A = 4
HD_A = D_MODEL // NH_A
N_GATE = 4 * NH_M
GATE_ROWS = 8
EPS = 1e-6
COL_QM, COL_KM, COL_VM, COL_OM, COL_ZM, COL_CB, COL_CC, COL_CX, COL_ZC, COL_QA, COL_ZA, COL_MG = range(12)
N_MAIN = 14 * D_MODEL
HALO = 16
VMEM_LIMIT = 56 * 1024 * 1024

f32 = jnp.float32
bf16 = jnp.bfloat16


def _sigmoid(x):
    return 1.0 / (1.0 + jnp.exp(-x))


def _nt_dot(a, b):
    return lax.dot_general(a, b, (((1,), (1,)), ((), ())), preferred_element_type=f32)


def _tn_dot(a, b):
    return lax.dot_general(a, b, (((0,), (0,)), ((), ())), preferred_element_type=f32)


def _norm_matmul_kernel(x_ref, g_ref, w_ref, *rest, with_gates):
    if with_gates:
        wgt_ref, bif_ref, p_ref, gt_ref, h_scr = rest
    else:
        p_ref, h_scr = rest

    @pl.when(pl.program_id(1) == 0)
    def _():
        xf = x_ref[...]
        ms = jnp.mean(xf * xf, axis=-1, keepdims=True)
        h = (xf * lax.rsqrt(ms + EPS) * g_ref[...]).astype(bf16)
        h_scr[...] = h
        if with_gates:
            gt_ref[...] = _nt_dot(wgt_ref[...], h) + bif_ref[...]

    p_ref[...] = jnp.dot(h_scr[...], w_ref[...], preferred_element_type=f32).astype(p_ref.dtype)


def _norm_matmul(x, g, w, wgt=None, bif=None, *, tm, tn):
    rows, d = x.shape
    n = w.shape[1]
    tm = min(tm, rows)
    tn = min(tn, n)
    with_gates = wgt is not None
    in_specs = [
        pl.BlockSpec((tm, d), lambda i, j: (i, 0)),
        pl.BlockSpec((1, d), lambda i, j: (0, 0)),
        pl.BlockSpec((d, tn), lambda i, j: (0, j)),
    ]
    out_shape = [jax.ShapeDtypeStruct((rows, n), bf16)]
    out_specs = [pl.BlockSpec((tm, tn), lambda i, j: (i, j))]
    args = [x, g, w]
    if with_gates:
        ng = wgt.shape[0]
        in_specs += [pl.BlockSpec((ng, d), lambda i, j: (0, 0)), pl.BlockSpec((ng, 1), lambda i, j: (0, 0))]
        out_shape.append(jax.ShapeDtypeStruct((ng, rows), f32))
        out_specs.append(pl.BlockSpec((ng, tm), lambda i, j: (0, i)))
        args += [wgt, bif]
    return pl.pallas_call(
        functools.partial(_norm_matmul_kernel, with_gates=with_gates),
        grid=(rows // tm, n // tn),
        in_specs=in_specs,
        out_specs=out_specs,
        out_shape=out_shape,
        scratch_shapes=[pltpu.VMEM((tm, d), bf16)],
        compiler_params=pltpu.CompilerParams(
            dimension_semantics=("parallel", "arbitrary"), vmem_limit_bytes=VMEM_LIMIT),
        name="norm_matmul_gates" if with_gates else "norm_matmul",
    )(*args)


def _lane_cumsum(x, reverse):
    lane = lax.broadcasted_iota(jnp.int32, x.shape, 1)
    n = x.shape[1]
    sh = 1
    while sh < n:
        if reverse:
            x = x + jnp.where(lane < n - sh, pltpu.roll(x, n - sh, axis=1), 0.0)
        else:
            x = x + jnp.where(lane >= sh, pltpu.roll(x, sh, axis=1), 0.0)
        sh *= 2
    return x


def _mlstm_chunk(qc, kc, vx, i_row, lf_row, c_scr, m, reverse):
    L, d = qc.shape
    r = lax.broadcasted_iota(jnp.int32, (L, L), 0)
    c = lax.broadcasted_iota(jnp.int32, (L, L), 1)
    causal = (c >= r) if reverse else (c <= r)
    eye = c == r

    b_row = _lane_cumsum(lf_row, reverse)
    b_col = jnp.sum(jnp.where(causal, lf_row, 0.0), axis=1, keepdims=True)
    w_row = i_row - b_row
    w_col = jnp.sum(jnp.where(eye, w_row, 0.0), axis=1, keepdims=True)
    bl = b_row[:, 0:1] if reverse else b_row[:, L - 1:L]

    dmat = jnp.where(causal, b_col + w_row, -jnp.inf)
    inter = b_col + m
    m_t = jnp.maximum(jnp.max(dmat, axis=1, keepdims=True), inter)
    s = _nt_dot(qc, kc) * jnp.exp(dmat - m_t)
    a = jnp.exp(inter - m_t)
    cx = c_scr[...]
    hx = (jnp.dot(s.astype(bf16), vx, preferred_element_type=f32)
          + a * jnp.dot(qc, cx.astype(bf16), preferred_element_type=f32))
    den = jnp.maximum(jnp.abs(hx[:, d:d + 1]), jnp.exp(-m_t))
    h = hx[:, :d] / den

    wlog = bl + w_col
    m_new = jnp.maximum(bl + m, jnp.max(wlog, axis=0, keepdims=True))
    kw = (kc.astype(f32) * jnp.exp(wlog - m_new)).astype(bf16)
    c_scr[...] = jnp.exp(bl + m - m_new) * cx + _tn_dot(kw, vx)
    return h, m_new


def _mlstm_kernel(q_ref, k_ref, v_ref, o_ref, z_ref, gt_ref, mhg_ref, y_ref, hf_scr, c_scr):
    S, d = q_ref.shape
    L = CHUNK
    nc = S // L
    ones_col = (lax.broadcasted_iota(jnp.int32, (L, 128), 1) == 0).astype(bf16)
    k_scale = HD_M ** -0.5

    def load(ci):
        rows = pl.ds(pl.multiple_of(ci * L, L), L)
        qc = q_ref[rows, :]
        kc = (k_ref[rows, :].astype(f32) * k_scale).astype(bf16)
        vx = jnp.concatenate([v_ref[rows, :], ones_col], axis=1)
        g8 = gt_ref[:, rows]
        lf8 = jnp.minimum(g8, 0.0) - jnp.log1p(jnp.exp(-jnp.abs(g8)))
        return rows, qc, kc, vx, g8, lf8

    c_scr[...] = jnp.zeros_like(c_scr)

    def fwd_body(ci, m):
        rows, qc, kc, vx, g8, lf8 = load(ci)
        h, m = _mlstm_chunk(qc, kc, vx, g8[0:1, :], lf8[2:3, :], c_scr, m, reverse=False)
        hf_scr[rows, :] = h
        return m

    lax.fori_loop(0, nc, fwd_body, jnp.zeros((1, 1), f32))

    c_scr[...] = jnp.zeros_like(c_scr)

    def bwd_body(it, m):
        ci = nc - 1 - it
        rows, qc, kc, vx, g8, lf8 = load(ci)
        h, m = _mlstm_chunk(qc, kc, vx, g8[1:2, :], lf8[3:4, :], c_scr, m, reverse=True)
        h = h + hf_scr[rows, :]
        mu = jnp.mean(h, axis=-1, keepdims=True)
        hc = h - mu
        var = jnp.mean(hc * hc, axis=-1, keepdims=True)
        hn = hc * lax.rsqrt(var + EPS) * mhg_ref[...]
        o = o_ref[rows, :].astype(f32)
        z = z_ref[rows, :].astype(f32)
        y_ref[rows, :] = (hn * _sigmoid(o) * (z * _sigmoid(z))).astype(y_ref.dtype)
        return m

    lax.fori_loop(0, nc, bwd_body, jnp.zeros((1, 1), f32))


def _mlstm(p, gt, mhg, *, batch, seq):
    def col(cb):
        return pl.BlockSpec((seq, HD_M), lambda b, h, cb=cb: (b, cb * NH_M + h))

    return pl.pallas_call(
        _mlstm_kernel,
        grid=(batch, NH_M),
        in_specs=[col(COL_QM), col(COL_KM), col(COL_VM), col(COL_OM), col(COL_ZM),
                  pl.BlockSpec((None, GATE_ROWS, seq), lambda b, h: (h, 0, b)),
                  pl.BlockSpec((1, HD_M), lambda b, h: (0, h))],
        out_specs=pl.BlockSpec((seq, HD_M), lambda b, h: (b, h)),
        out_shape=jax.ShapeDtypeStruct((batch * seq, D_MODEL), bf16),
        scratch_shapes=[pltpu.VMEM((seq, HD_M), f32), pltpu.VMEM((HD_M, HD_M + 128), f32)],
        compiler_params=pltpu.CompilerParams(
            dimension_semantics=("parallel", "parallel"), vmem_limit_bytes=VMEM_LIMIT),
        name="mlstm",
    )(p, p, p, p, p, gt, mhg)


def _branches_kernel(cb_ref, cc_ref, cx_ref, zc_ref, ccp_ref, cxp_ref, ccn_ref, cxn_ref,
                     qa_ref, za_ref, mg0_ref, mg1_ref, mg2_ref, ym_ref, km_ref, vm_ref,
                     convw_ref, wb_ref, out_ref):
    tm = cb_ref.shape[0]
    i = pl.program_id(1)
    ni = pl.num_programs(1)

    u = cc_ref[...].astype(f32) * cx_ref[...].astype(f32)
    u_before = (ccp_ref[HALO - 1:HALO, :].astype(f32) * cxp_ref[HALO - 1:HALO, :].astype(f32)
                * (i > 0).astype(f32))
    u_after = ccn_ref[0:1, :].astype(f32) * cxn_ref[0:1, :].astype(f32) * (i < ni - 1).astype(f32)
    row = lax.broadcasted_iota(jnp.int32, (tm, 1), 0)
    u_prev = jnp.where(row == 0, u_before, pltpu.roll(u, 1, axis=0))
    u_next = jnp.where(row == tm - 1, u_after, pltpu.roll(u, tm - 1, axis=0))
    w = convw_ref[...]
    conv = w[0:1, :] * u_prev + w[1:2, :] * u + w[2:3, :] * u_next
    zc = zc_ref[...].astype(f32)
    y_c = cb_ref[...].astype(f32) * conv * (zc * _sigmoid(zc))

    heads = []
    for hd in range(NH_A):
        cols = slice(hd * HD_A, (hd + 1) * HD_A)
        s = _nt_dot(qa_ref[:, cols], km_ref[:, cols]) * (HD_A ** -0.5)
        s = s - jnp.max(s, axis=-1, keepdims=True)
        e = jnp.exp(s)
        p = e / jnp.sum(e, axis=-1, keepdims=True)
        heads.append(jnp.dot(p.astype(bf16), vm_ref[:, cols], preferred_element_type=f32))
    za = za_ref[...].astype(f32)
    y_a = jnp.concatenate(heads, axis=1) * (za * _sigmoid(za))

    def proj(y, k):
        return jnp.dot(y.astype(bf16), wb_ref[k], preferred_element_type=f32)

    merged = (_sigmoid(mg0_ref[...].astype(f32)) * proj(ym_ref[...], 0)
              + _sigmoid(mg1_ref[...].astype(f32)) * proj(y_c, 1)
              + _sigmoid(mg2_ref[...].astype(f32)) * proj(y_a, 2))
    out_ref[...] = merged.astype(out_ref.dtype)


def _branches(p, ym, kv, convw, wb, *, batch, seq, tm):
    tm = min(tm, seq)
    nb = seq // tm
    hb = tm // HALO
    last_halo = batch * seq // HALO - 1

    def col(cb):
        return pl.BlockSpec((tm, D_MODEL), lambda b, i, cb=cb: (b * nb + i, cb))

    def halo_before(cb):
        return pl.BlockSpec((HALO, D_MODEL), lambda b, i, cb=cb: (jnp.maximum((b * nb + i) * hb - 1, 0), cb))

    def halo_after(cb):
        return pl.BlockSpec((HALO, D_MODEL),
                            lambda b, i, cb=cb: (jnp.minimum((b * nb + i + 1) * hb, last_halo), cb))

    n_mem = kv.shape[0] // batch
    const = lambda *shape: pl.BlockSpec(shape, lambda b, i: (0,) * len(shape), pipeline_mode=pl.Buffered(1))
    return pl.pallas_call(
        _branches_kernel,
        grid=(batch, nb),
        in_specs=[col(COL_CB), col(COL_CC), col(COL_CX), col(COL_ZC),
                  halo_before(COL_CC), halo_before(COL_CX), halo_after(COL_CC), halo_after(COL_CX),
                  col(COL_QA), col(COL_ZA), col(COL_MG), col(COL_MG + 1), col(COL_MG + 2),
                  pl.BlockSpec((tm, D_MODEL), lambda b, i: (b * nb + i, 0)),
                  pl.BlockSpec((n_mem, D_MODEL), lambda b, i: (b, 0)),
                  pl.BlockSpec((n_mem, D_MODEL), lambda b, i: (b, 1)),
                  const(3, D_MODEL), const(3, D_MODEL, D_MODEL)],
        out_specs=pl.BlockSpec((tm, D_MODEL), lambda b, i: (b * nb + i, 0)),
        out_shape=jax.ShapeDtypeStruct((batch * seq, D_MODEL), bf16),
        compiler_params=pltpu.CompilerParams(
            dimension_semantics=("parallel", "arbitrary"), vmem_limit_bytes=VMEM_LIMIT),
        name="branches",
    )(p, p, p, p, p, p, p, p, p, p, p, p, p, ym, kv, kv, convw, wb)


def _out_proj_kernel(x_ref, mrg_ref, w_ref, g_ref, y_ref):
    r = x_ref[...] + jnp.dot(mrg_ref[...], w_ref[...], preferred_element_type=f32)
    ms = jnp.mean(r * r, axis=-1, keepdims=True)
    y_ref[...] = r * lax.rsqrt(ms + EPS) * g_ref[...]


def _out_proj(x, mrg, w, g, *, tm):
    rows, d = x.shape
    tm = min(tm, rows)
    return pl.pallas_call(
        _out_proj_kernel,
        grid=(rows // tm,),
        in_specs=[pl.BlockSpec((tm, d), lambda i: (i, 0)),
                  pl.BlockSpec((tm, d), lambda i: (i, 0)),
                  pl.BlockSpec((d, d), lambda i: (0, 0)),
                  pl.BlockSpec((1, d), lambda i: (0, 0))],
        out_specs=pl.BlockSpec((tm, d), lambda i: (i, 0)),
        out_shape=jax.ShapeDtypeStruct((rows, d), f32),
        compiler_params=pltpu.CompilerParams(
            dimension_semantics=("parallel",), vmem_limit_bytes=VMEM_LIMIT),
        name="out_proj",
    )(x, mrg, w, g)


def _trunk(x, mem, wts):
    batch, seq, d = x.shape
    x2 = x.reshape(batch * seq, d)
    mem2 = mem.reshape(-1, d)
    p, gt = _norm_matmul(x2, wts["norm_g"], wts["w_main"], wts["w_gate_t"], wts["b_gate"], tm=1024, tn=1024)
    kv, = _norm_matmul(mem2, wts["mem_g"], wts["w_kv"], tm=1024, tn=1024)
    gt = gt.reshape(NH_M, GATE_ROWS, batch * seq)
    ym = _mlstm(p, gt, wts["mh_g"], batch=batch, seq=seq)
    mrg = _branches(p, ym, kv, wts["conv_w"], wts["w_branch"], batch=batch, seq=seq, tm=128)
    y = _out_proj(x2, mrg, wts["w_out"], wts["final_g"], tm=512)
    return y.reshape(batch, seq, d)


def kernel(x_prompt, x_sample, mem_prompt, mem_sample, norm_g, w_in, b_if, conv_w, mem_norm_g, w_kv_mem,
           mh_norm_g, w_branch, w_out, final_norm_g):
    assert norm_g.shape[0] == 1, "single-layer trunk"
    w = w_in[0]
    g0 = 5 * D_MODEL
    w_main = jnp.concatenate([w[:, :g0], w[:, g0 + N_GATE:]], axis=1).astype(bf16)
    wg = w[:, g0:g0 + N_GATE].reshape(D_MODEL, 4, NH_M).transpose(2, 1, 0)
    wg = jnp.pad(wg, ((0, 0), (0, GATE_ROWS - 4), (0, 0))).reshape(NH_M * GATE_ROWS, D_MODEL).astype(bf16)
    bg = jnp.pad(b_if[0].reshape(4, NH_M).T, ((0, 0), (0, GATE_ROWS - 4))).reshape(NH_M * GATE_ROWS, 1)
    wts = dict(
        norm_g=norm_g[0][None, :], w_main=w_main, w_gate_t=wg, b_gate=bg.astype(f32),
        mem_g=mem_norm_g[0][None, :], w_kv=w_kv_mem[0].astype(bf16), mh_g=mh_norm_g[0][None, :],
        conv_w=conv_w[0], w_branch=w_branch[0].astype(bf16), w_out=w_out[0].astype(bf16),
        final_g=final_norm_g[None, :])
    return (_trunk(x_prompt, mem_prompt, wts), _trunk(x_sample, mem_sample, wts))
```

```python
import functools

import jax
import jax.numpy as jnp
from jax import lax
from jax.experimental import pallas as pl
from jax.experimental.pallas import tpu as pltpu

D_MODEL = 2048
NH_M = 8
HD_M = D_MODEL // NH_M
CHUNK = 128
NH_A = 4
HD_A = D_MODEL // NH_A
N_GATE = 4 * NH_M
GATE_ROWS = 8
EPS = 1e-6
COL_QM, COL_KM, COL_VM, COL_OM, COL_ZM, COL_CB, COL_CC, COL_CX, COL_ZC, COL_QA, COL_ZA, COL_MG = range(12)
N_MAIN = 14 * D_MODEL
HALO = 16
VMEM_LIMIT = 56 * 1024 * 1024

f32 = jnp.float32
bf16 = jnp.bfloat16


def _sigmoid(x):
    return 1.0 / (1.0 + jnp.exp(-x))


def _nt_dot(a, b):
    return lax.dot_general(a, b, (((1,), (1,)), ((), ())), preferred_element_type=f32)


def _tn_dot(a, b):
    return lax.dot_general(a, b, (((0,), (0,)), ((), ())), preferred_element_type=f32)


def _norm_matmul_kernel(x_ref, g_ref, w_ref, *rest, with_gates):
    if with_gates:
        wgt_ref, bif_ref, p_ref, gt_ref, h_scr = rest
    else:
        p_ref, h_scr = rest

    @pl.when(pl.program_id(1) == 0)
    def _():
        xf = x_ref[...]
        ms = jnp.mean(xf * xf, axis=-1, keepdims=True)
        h = (xf * lax.rsqrt(ms + EPS) * g_ref[...]).astype(bf16)
        h_scr[...] = h
        if with_gates:
            gt_ref[...] = _nt_dot(wgt_ref[...], h) + bif_ref[...]

    p_ref[...] = jnp.dot(h_scr[...], w_ref[...], preferred_element_type=f32).astype(p_ref.dtype)


def _norm_matmul(x, g, w, wgt=None, bif=None, *, tm, tn):
    rows, d = x.shape
    n = w.shape[1]
    tm = min(tm, rows)
    tn = min(tn, n)
    with_gates = wgt is not None
    in_specs = [
        pl.BlockSpec((tm, d), lambda i, j: (i, 0)),
        pl.BlockSpec((1, d), lambda i, j: (0, 0)),
        pl.BlockSpec((d, tn), lambda i, j: (0, j)),
    ]
    out_shape = [jax.ShapeDtypeStruct((rows, n), bf16)]
    out_specs = [pl.BlockSpec((tm, tn), lambda i, j: (i, j))]
    args = [x, g, w]
    if with_gates:
        ng = wgt.shape[0]
        in_specs += [pl.BlockSpec((ng, d), lambda i, j: (0, 0)), pl.BlockSpec((ng, 1), lambda i, j: (0, 0))]
        out_shape.append(jax.ShapeDtypeStruct((ng, rows), f32))
        out_specs.append(pl.BlockSpec((ng, tm), lambda i, j: (0, i)))
        args += [wgt, bif]
    return pl.pallas_call(
        functools.partial(_norm_matmul_kernel, with_gates=with_gates),
        grid=(rows // tm, n // tn),
        in_specs=in_specs,
        out_specs=out_specs,
        out_shape=out_shape,
        scratch_shapes=[pltpu.VMEM((tm, d), bf16)],
        compiler_params=pltpu.CompilerParams(
            dimension_semantics=("parallel", "arbitrary"), vmem_limit_bytes=VMEM_LIMIT),
        name="norm_matmul_gates" if with_gates else "norm_matmul",
    )(*args)


def _lane_scan(x, op, reverse):
    lane = lax.broadcasted_iota(jnp.int32, x.shape, 1)
    n = x.shape[1]
    sh = 1
    while sh < n:
        if reverse:
            x = jnp.where(lane < n - sh, op(x, pltpu.roll(x, n - sh, axis=1)), x)
        else:
            x = jnp.where(lane >= sh, op(x, pltpu.roll(x, sh, axis=1)), x)
        sh *= 2
    return x


def _col_bcast(row):
    L = row.shape[1]
    return jnp.broadcast_to(row, (L, L)).T


def _mlstm_kernel(q_ref, k_ref, v_ref, o_ref, z_ref, gt_ref, mhg_ref, y_ref,
                  w_scr, b_scr, cw_scr, tot_scr, mw_scr, h_scr, c_scr):
    S, d = q_ref.shape
    L = CHUNK
    nc = S // L
    k_scale = HD_M ** -0.5

    i_f, i_b, f_f, f_b = gt_ref[0], gt_ref[1], gt_ref[2], gt_ref[3]
    for dr, (i_pre, f_pre) in enumerate(((i_f, f_f), (i_b, f_b))):
        rev = dr == 1
        lf = jnp.minimum(f_pre, 0.0) - jnp.log1p(jnp.exp(-jnp.abs(f_pre)))
        b = _lane_scan(lf, jnp.add, rev)
        tot = b + _lane_scan(lf, jnp.add, not rev) - lf
        w = i_pre - b
        cw = _lane_scan(w, jnp.maximum, rev)
        mw = jnp.maximum(cw, _lane_scan(w, jnp.maximum, not rev))
        w_scr[dr] = w
        b_scr[dr] = b
        cw_scr[dr] = cw
        tot_scr[dr] = tot
        mw_scr[dr] = mw

    r = lax.broadcasted_iota(jnp.int32, (L, L), 0)
    c = lax.broadcasted_iota(jnp.int32, (L, L), 1)
    ones_tile = jnp.ones((L, 128), bf16)
    c_scr[...] = jnp.zeros_like(c_scr)

    def chunk(ci, dr, m):
        causal = (c >= r) if dr == 1 else (c <= r)
        rows = pl.ds(pl.multiple_of(ci * L, L), L)
        one = pl.ds(ci, 1)
        w_row, tot, mw = w_scr[dr, one, :], tot_scr[dr, one, :], mw_scr[dr, one, :]
        qc = q_ref[rows, :]
        kt = (k_ref[rows, :].astype(f32) * k_scale).T
        vx = jnp.concatenate([v_ref[rows, :], ones_tile], axis=1)
        g = jnp.maximum(_col_bcast(cw_scr[dr, one, :]), m)
        e = jnp.exp(jnp.where(causal, w_row - g, -jnp.inf))
        a = jnp.exp(m - g)
        s = jnp.dot(qc, kt.astype(bf16), preferred_element_type=f32) * e
        cx = c_scr[dr]
        hx = (jnp.dot(s.astype(bf16), vx, preferred_element_type=f32)
              + jnp.concatenate([a, a, a], axis=1) * jnp.dot(qc, cx.astype(bf16), preferred_element_type=f32))
        den = jnp.maximum(jnp.abs(hx[:, d:]), jnp.exp(-(g + _col_bcast(b_scr[dr, one, :]))))
        rden = 1.0 / den
        h_scr[dr, rows, :] = hx[:, :d] * jnp.concatenate([rden, rden], axis=1)

        m_up = jnp.maximum(m, mw)
        m_new = tot + m_up
        kw = (kt * jnp.exp(tot + w_row - m_new)).astype(bf16)
        decay = jnp.exp(m - m_up)
        c_scr[dr] = (jnp.concatenate([decay, decay, decay], axis=1) * cx
                     + jnp.dot(kw, vx, preferred_element_type=f32))
        return m_new

    def body(it, ms):
        return chunk(it, 0, ms[0]), chunk(nc - 1 - it, 1, ms[1])

    m0 = jnp.zeros((1, L), f32)
    lax.fori_loop(0, nc, body, (m0, m0))

    def finish(ci, carry):
        rows = pl.ds(pl.multiple_of(ci * L, L), L)
        h = h_scr[0, rows, :] + h_scr[1, rows, :]
        mu = jnp.mean(h, axis=-1, keepdims=True)
        hc = h - mu
        var = jnp.mean(hc * hc, axis=-1, keepdims=True)
        hn = hc * lax.rsqrt(var + EPS) * mhg_ref[...]
        o = o_ref[rows, :].astype(f32)
        z = z_ref[rows, :].astype(f32)
        y_ref[rows, :] = (hn * _sigmoid(o) * (z * _sigmoid(z))).astype(y_ref.dtype)
        return carry

    lax.fori_loop(0, nc, finish, 0)


def _mlstm(p, gt, mhg, *, batch, seq):
    nc = seq // CHUNK

    def col(cb):
        return pl.BlockSpec((seq, HD_M), lambda b, h, cb=cb: (b, cb * NH_M + h))

    gate_rows = pltpu.VMEM((2, nc, CHUNK), f32)
    return pl.pallas_call(
        _mlstm_kernel,
        grid=(batch, NH_M),
        in_specs=[col(COL_QM), col(COL_KM), col(COL_VM), col(COL_OM), col(COL_ZM),
                  pl.BlockSpec((None, None, 4, nc, CHUNK), lambda b, h: (b, h, 0, 0, 0)),
                  pl.BlockSpec((1, HD_M), lambda b, h: (0, h))],
        out_specs=pl.BlockSpec((seq, HD_M), lambda b, h: (b, h)),
        out_shape=jax.ShapeDtypeStruct((batch * seq, D_MODEL), bf16),
        scratch_shapes=[gate_rows, gate_rows, gate_rows, gate_rows, gate_rows,
                        pltpu.VMEM((2, seq, HD_M), f32), pltpu.VMEM((2, HD_M, HD_M + 128), f32)],
        compiler_params=pltpu.CompilerParams(
            dimension_semantics=("parallel", "parallel"), vmem_limit_bytes=VMEM_LIMIT),
        name="mlstm",
    )(p, p, p, p, p, gt, mhg)


def _branches_kernel(cb_ref, cc_ref, cx_ref, zc_ref, ccp_ref, cxp_ref, ccn_ref, cxn_ref,
                     qa_ref, za_ref, mg0_ref, mg1_ref, mg2_ref, ym_ref, km_ref, vm_ref,
                     convw_ref, wb_ref, out_ref):
    tm = cb_ref.shape[0]
    i = pl.program_id(1)
    ni = pl.num_programs(1)

    u = cc_ref[...].astype(f32) * cx_ref[...].astype(f32)
    u_before = (ccp_ref[HALO - 1:HALO, :].astype(f32) * cxp_ref[HALO - 1:HALO, :].astype(f32)
                * (i > 0).astype(f32))
    u_after = ccn_ref[0:1, :].astype(f32) * cxn_ref[0:1, :].astype(f32) * (i < ni - 1).astype(f32)
    row = lax.broadcasted_iota(jnp.int32, (tm, 1), 0)
    u_prev = jnp.where(row == 0, u_before, pltpu.roll(u, 1, axis=0))
    u_next = jnp.where(row == tm - 1, u_after, pltpu.roll(u, tm - 1, axis=0))
    w = convw_ref[...]
    conv = w[0:1, :] * u_prev + w[1:2, :] * u + w[2:3, :] * u_next
    zc = zc_ref[...].astype(f32)
    y_c = cb_ref[...].astype(f32) * conv * (zc * _sigmoid(zc))

    heads = []
    for hd in range(NH_A):
        cols = slice(hd * HD_A, (hd + 1) * HD_A)
        s = _nt_dot(qa_ref[:, cols], km_ref[:, cols]) * (HD_A ** -0.5)
        s = s - jnp.max(s, axis=-1, keepdims=True)
        e = jnp.exp(s)
        p = e / jnp.sum(e, axis=-1, keepdims=True)
        heads.append(jnp.dot(p.astype(bf16), vm_ref[:, cols], preferred_element_type=f32))
    za = za_ref[...].astype(f32)
    y_a = jnp.concatenate(heads, axis=1) * (za * _sigmoid(za))

    def proj(y, k):
        return jnp.dot(y.astype(bf16), wb_ref[k], preferred_element_type=f32)

    merged = (_sigmoid(mg0_ref[...].astype(f32)) * proj(ym_ref[...], 0)
              + _sigmoid(mg1_ref[...].astype(f32)) * proj(y_c, 1)
              + _sigmoid(mg2_ref[...].astype(f32)) * proj(y_a, 2))
    out_ref[...] = merged.astype(out_ref.dtype)


def _branches(p, ym, kv, convw, wb, *, batch, seq, tm):
    tm = min(tm, seq)
    nb = seq // tm
    hb = tm // HALO
    last_halo = batch * seq // HALO - 1

    def col(cb):
        return pl.BlockSpec((tm, D_MODEL), lambda b, i, cb=cb: (b * nb + i, cb))

    def halo_before(cb):
        return pl.BlockSpec((HALO, D_MODEL), lambda b, i, cb=cb: (jnp.maximum((b * nb + i) * hb - 1, 0), cb))

    def halo_after(cb):
        return pl.BlockSpec((HALO, D_MODEL),
                            lambda b, i, cb=cb: (jnp.minimum((b * nb + i + 1) * hb, last_halo), cb))

    n_mem = kv.shape[0] // batch
    const = lambda *shape: pl.BlockSpec(shape, lambda b, i: (0,) * len(shape), pipeline_mode=pl.Buffered(1))
    return pl.pallas_call(
        _branches_kernel,
        grid=(batch, nb),
        in_specs=[col(COL_CB), col(COL_CC), col(COL_CX), col(COL_ZC),
                  halo_before(COL_CC), halo_before(COL_CX), halo_after(COL_CC), halo_after(COL_CX),
                  col(COL_QA), col(COL_ZA), col(COL_MG), col(COL_MG + 1), col(COL_MG + 2),
                  pl.BlockSpec((tm, D_MODEL), lambda b, i: (b * nb + i, 0)),
                  pl.BlockSpec((n_mem, D_MODEL), lambda b, i: (b, 0)),
                  pl.BlockSpec((n_mem, D_MODEL), lambda b, i: (b, 1)),
                  const(3, D_MODEL), const(3, D_MODEL, D_MODEL)],
        out_specs=pl.BlockSpec((tm, D_MODEL), lambda b, i: (b * nb + i, 0)),
        out_shape=jax.ShapeDtypeStruct((batch * seq, D_MODEL), bf16),
        compiler_params=pltpu.CompilerParams(
            dimension_semantics=("parallel", "arbitrary"), vmem_limit_bytes=VMEM_LIMIT),
        name="branches",
    )(p, p, p, p, p, p, p, p, p, p, p, p, p, ym, kv, kv, convw, wb)


def _out_proj_kernel(x_ref, mrg_ref, w_ref, g_ref, y_ref):
    r = x_ref[...] + jnp.dot(mrg_ref[...], w_ref[...], preferred_element_type=f32)
    ms = jnp.mean(r * r, axis=-1, keepdims=True)
    y_ref[...] = r * lax.rsqrt(ms + EPS) * g_ref[...]


def _out_proj(x, mrg, w, g, *, tm):
    rows, d = x.shape
    tm = min(tm, rows)
    return pl.pallas_call(
        _out_proj_kernel,
        grid=(rows // tm,),
        in_specs=[pl.BlockSpec((tm, d), lambda i: (i, 0)),
                  pl.BlockSpec((tm, d), lambda i: (i, 0)),
                  pl.BlockSpec((d, d), lambda i: (0, 0)),
                  pl.BlockSpec((1, d), lambda i: (0, 0))],
        out_specs=pl.BlockSpec((tm, d), lambda i: (i, 0)),
        out_shape=jax.ShapeDtypeStruct((rows, d), f32),
        compiler_params=pltpu.CompilerParams(
            dimension_semantics=("parallel",), vmem_limit_bytes=VMEM_LIMIT),
        name="out_proj",
    )(x, mrg, w, g)


def _trunk(x, mem, wts):
    batch, seq, d = x.shape
    x2 = x.reshape(batch * seq, d)
    mem2 = mem.reshape(-1, d)
    p, gt = _norm_matmul(x2, wts["norm_g"], wts["w_main"], wts["w_gate_t"], wts["b_gate"], tm=1024, tn=1024)
    kv, = _norm_matmul(mem2, wts["mem_g"], wts["w_kv"], tm=1024, tn=1024)
    gt = gt.reshape(NH_M, GATE_ROWS, batch, seq // CHUNK, CHUNK)[:, :4].transpose(2, 0, 1, 3, 4)
    ym = _mlstm(p, gt, wts["mh_g"], batch=batch, seq=seq)
    mrg = _branches(p, ym, kv, wts["conv_w"], wts["w_branch"], batch=batch, seq=seq, tm=128)
    y = _out_proj(x2, mrg, wts["w_out"], wts["final_g"], tm=512)
    return y.reshape(batch, seq, d)


def kernel(x_prompt, x_sample, mem_prompt, mem_sample, norm_g, w_in, b_if, conv_w, mem_norm_g, w_kv_mem,
           mh_norm_g, w_branch, w_out, final_norm_g):
    assert norm_g.shape[0] == 1, "single-layer trunk"
    w = w_in[0]
    g0 = 5 * D_MODEL
    w_main = jnp.concatenate([w[:, :g0], w[:, g0 + N_GATE:]], axis=1).astype(bf16)
    wg = w[:, g0:g0 + N_GATE].reshape(D_MODEL, 4, NH_M).transpose(2, 1, 0)
    wg = jnp.pad(wg, ((0, 0), (0, GATE_ROWS - 4), (0, 0))).reshape(NH_M * GATE_ROWS, D_MODEL).astype(bf16)
    bg = jnp.pad(b_if[0].reshape(4, NH_M).T, ((0, 0), (0, GATE_ROWS - 4))).reshape(NH_M * GATE_ROWS, 1)
    wts = dict(
        norm_g=norm_g[0][None, :], w_main=w_main, w_gate_t=wg, b_gate=bg.astype(f32),
        mem_g=mem_norm_g[0][None, :], w_kv=w_kv_mem[0].astype(bf16), mh_g=mh_norm_g[0][None, :],
        conv_w=conv_w[0], w_branch=w_branch[0].astype(bf16), w_out=w_out[0].astype(bf16),
        final_g=final_norm_g[None, :])
    return (_trunk(x_prompt, mem_prompt, wts), _trunk(x_sample, mem_sample, wts))
```

```python
import functools

import jax
import jax.numpy as jnp
from jax import lax
from jax.experimental import pallas as pl
from jax.experimental.pallas import tpu as pltpu

D_MODEL = 2048
NH_M = 8
HD_M = D_MODEL // NH_M
CHUNK = 128
NH_A = 4
HD_A = D_MODEL // NH_A
N_GATE = 4 * NH_M
GATE_ROWS = 8
EPS = 1e-6
COL_QM, COL_KM, COL_VM, COL_OM, COL_ZM, COL_CB, COL_CC, COL_CX, COL_ZC, COL_QA, COL_ZA, COL_MG = range(12)
N_MAIN = 14 * D_MODEL
HALO = 16
VMEM_LIMIT = 56 * 1024 * 1024

f32 = jnp.float32
bf16 = jnp.bfloat16


def _sigmoid(x):
    return 0.5 * jnp.tanh(0.5 * x) + 0.5


def _nt_dot(a, b):
    return lax.dot_general(a, b, (((1,), (1,)), ((), ())), preferred_element_type=f32)


def _tn_dot(a, b):
    return lax.dot_general(a, b, (((0,), (0,)), ((), ())), preferred_element_type=f32)


def _norm_matmul_kernel(x_ref, g_ref, w_ref, *rest, with_gates):
    if with_gates:
        wgt_ref, bif_ref, p_ref, gt_ref, h_scr = rest
    else:
        p_ref, h_scr = rest

    @pl.when(pl.program_id(1) == 0)
    def _():
        xf = x_ref[...]
        ms = jnp.mean(xf * xf, axis=-1, keepdims=True)
        h = (xf * lax.rsqrt(ms + EPS) * g_ref[...]).astype(bf16)
        h_scr[...] = h
        if with_gates:
            gt_ref[...] = _nt_dot(wgt_ref[...], h) + bif_ref[...]

    p_ref[...] = jnp.dot(h_scr[...], w_ref[...], preferred_element_type=f32).astype(p_ref.dtype)


def _norm_matmul(x, g, w, wgt=None, bif=None, *, tm, tn):
    rows, d = x.shape
    n = w.shape[1]
    tm = min(tm, rows)
    tn = min(tn, n)
    with_gates = wgt is not None
    in_specs = [
        pl.BlockSpec((tm, d), lambda i, j: (i, 0)),
        pl.BlockSpec((1, d), lambda i, j: (0, 0)),
        pl.BlockSpec((d, tn), lambda i, j: (0, j)),
    ]
    out_shape = [jax.ShapeDtypeStruct((rows, n), bf16)]
    out_specs = [pl.BlockSpec((tm, tn), lambda i, j: (i, j))]
    args = [x, g, w]
    if with_gates:
        ng = wgt.shape[0]
        in_specs += [pl.BlockSpec((ng, d), lambda i, j: (0, 0)), pl.BlockSpec((ng, 1), lambda i, j: (0, 0))]
        out_shape.append(jax.ShapeDtypeStruct((ng, rows), f32))
        out_specs.append(pl.BlockSpec((ng, tm), lambda i, j: (0, i)))
        args += [wgt, bif]
    return pl.pallas_call(
        functools.partial(_norm_matmul_kernel, with_gates=with_gates),
        grid=(rows // tm, n // tn),
        in_specs=in_specs,
        out_specs=out_specs,
        out_shape=out_shape,
        scratch_shapes=[pltpu.VMEM((tm, d), bf16)],
        compiler_params=pltpu.CompilerParams(
            dimension_semantics=("arbitrary", "arbitrary"), vmem_limit_bytes=VMEM_LIMIT),
        name="norm_matmul_gates" if with_gates else "norm_matmul",
    )(*args)


def _lane_scan(x, op, reverse):
    lane = lax.broadcasted_iota(jnp.int32, x.shape, 1)
    n = x.shape[1]
    sh = 1
    while sh < n:
        if reverse:
            x = jnp.where(lane < n - sh, op(x, pltpu.roll(x, n - sh, axis=1)), x)
        else:
            x = jnp.where(lane >= sh, op(x, pltpu.roll(x, sh, axis=1)), x)
        sh *= 2
    return x


def _col_bcast(row):
    L = row.shape[1]
    return jnp.broadcast_to(row, (L, L)).T


def _mlstm_kernel(q_ref, k_ref, v_ref, o_ref, z_ref, gt_ref, mhg_ref, y_ref,
                  w_scr, b_scr, cw_scr, tot_scr, mw_scr, h_scr, c_scr):
    S, d = q_ref.shape
    L = CHUNK
    nc = S // L
    k_scale = HD_M ** -0.5

    i_f, i_b, f_f, f_b = gt_ref[0], gt_ref[1], gt_ref[2], gt_ref[3]
    for dr, (i_pre, f_pre) in enumerate(((i_f, f_f), (i_b, f_b))):
        rev = dr == 1
        lf = jnp.minimum(f_pre, 0.0) - jnp.log1p(jnp.exp(-jnp.abs(f_pre)))
        b = _lane_scan(lf, jnp.add, rev)
        tot = b + _lane_scan(lf, jnp.add, not rev) - lf
        w = i_pre - b
        cw = _lane_scan(w, jnp.maximum, rev)
        mw = jnp.maximum(cw, _lane_scan(w, jnp.maximum, not rev))
        w_scr[dr] = w
        b_scr[dr] = b
        cw_scr[dr] = cw
        tot_scr[dr] = tot
        mw_scr[dr] = mw

    r = lax.broadcasted_iota(jnp.int32, (L, L), 0)
    c = lax.broadcasted_iota(jnp.int32, (L, L), 1)
    ones_tile = jnp.ones((L, 128), bf16)
    c_scr[...] = jnp.zeros_like(c_scr)

    def chunk(ci, dr, m):
        causal = (c >= r) if dr == 1 else (c <= r)
        rows = pl.ds(pl.multiple_of(ci * L, L), L)
        one = pl.ds(ci, 1)
        w_row, tot, mw = w_scr[dr, one, :], tot_scr[dr, one, :], mw_scr[dr, one, :]
        qc = q_ref[rows, :]
        kt = (k_ref[rows, :].astype(f32) * k_scale).T
        vx = jnp.concatenate([v_ref[rows, :], ones_tile], axis=1)
        g = jnp.maximum(_col_bcast(cw_scr[dr, one, :]), m)
        e = jnp.exp(jnp.where(causal, w_row - g, -jnp.inf))
        a = jnp.exp(m - g)
        s = jnp.dot(qc, kt.astype(bf16), preferred_element_type=f32) * e
        cx = c_scr[dr]
        hx = (jnp.dot(s.astype(bf16), vx, preferred_element_type=f32)
              + jnp.concatenate([a, a, a], axis=1) * jnp.dot(qc, cx.astype(bf16), preferred_element_type=f32))
        den = jnp.maximum(jnp.abs(hx[:, d:]), jnp.exp(-(g + _col_bcast(b_scr[dr, one, :]))))
        rden = 1.0 / den
        h_scr[dr, rows, :] = hx[:, :d] * jnp.concatenate([rden, rden], axis=1)

        m_up = jnp.maximum(m, mw)
        m_new = tot + m_up
        kw = (kt * jnp.exp(tot + w_row - m_new)).astype(bf16)
        decay = jnp.exp(m - m_up)
        c_scr[dr] = (jnp.concatenate([decay, decay, decay], axis=1) * cx
                     + jnp.dot(kw, vx, preferred_element_type=f32))
        return m_new

    def body(it, ms):
        return chunk(it, 0, ms[0]), chunk(nc - 1 - it, 1, ms[1])

    m0 = jnp.zeros((1, L), f32)
    unroll = 2 if nc % 2 == 0 else 1
    lax.fori_loop(0, nc, body, (m0, m0), unroll=unroll)

    def finish(ci, carry):
        rows = pl.ds(pl.multiple_of(ci * L, L), L)
        h = h_scr[0, rows, :] + h_scr[1, rows, :]
        mu = jnp.mean(h, axis=-1, keepdims=True)
        hc = h - mu
        var = jnp.mean(hc * hc, axis=-1, keepdims=True)
        hn = hc * lax.rsqrt(var + EPS) * mhg_ref[...]
        o = o_ref[rows, :].astype(f32)
        z = z_ref[rows, :].astype(f32)
        y_ref[rows, :] = (hn * _sigmoid(o) * (z * _sigmoid(z))).astype(y_ref.dtype)
        return carry

    lax.fori_loop(0, nc, finish, 0, unroll=unroll)


def _mlstm(p, gt, mhg, *, batch, seq):
    nc = seq // CHUNK

    def col(cb):
        return pl.BlockSpec((seq, HD_M), lambda b, h, cb=cb: (b, cb * NH_M + h))

    gate_rows = pltpu.VMEM((2, nc, CHUNK), f32)
    return pl.pallas_call(
        _mlstm_kernel,
        grid=(batch, NH_M),
        in_specs=[col(COL_QM), col(COL_KM), col(COL_VM), col(COL_OM), col(COL_ZM),
                  pl.BlockSpec((None, None, 4, nc, CHUNK), lambda b, h: (b, h, 0, 0, 0)),
                  pl.BlockSpec((1, HD_M), lambda b, h: (0, h))],
        out_specs=pl.BlockSpec((seq, HD_M), lambda b, h: (b, h)),
        out_shape=jax.ShapeDtypeStruct((batch * seq, D_MODEL), bf16),
        scratch_shapes=[gate_rows, gate_rows, gate_rows, gate_rows, gate_rows,
                        pltpu.VMEM((2, seq, HD_M), f32), pltpu.VMEM((2, HD_M, HD_M + 128), f32)],
        compiler_params=pltpu.CompilerParams(
            dimension_semantics=("arbitrary", "arbitrary"), vmem_limit_bytes=VMEM_LIMIT),
        name="mlstm",
    )(p, p, p, p, p, gt, mhg)


def _branches_kernel(cb_ref, cc_ref, cx_ref, zc_ref, ccp_ref, cxp_ref, ccn_ref, cxn_ref,
                     qa_ref, za_ref, mg0_ref, mg1_ref, mg2_ref, ym_ref, km_ref, vm_ref,
                     convw_ref, wb_ref, out_ref):
    tm = cb_ref.shape[0]
    i = pl.program_id(1)
    ni = pl.num_programs(1)

    u = cc_ref[...].astype(f32) * cx_ref[...].astype(f32)
    u_before = (ccp_ref[HALO - 1:HALO, :].astype(f32) * cxp_ref[HALO - 1:HALO, :].astype(f32)
                * (i > 0).astype(f32))
    u_after = ccn_ref[0:1, :].astype(f32) * cxn_ref[0:1, :].astype(f32) * (i < ni - 1).astype(f32)
    row = lax.broadcasted_iota(jnp.int32, (tm, 1), 0)
    u_prev = jnp.where(row == 0, u_before, pltpu.roll(u, 1, axis=0))
    u_next = jnp.where(row == tm - 1, u_after, pltpu.roll(u, tm - 1, axis=0))
    w = convw_ref[...]
    conv = w[0:1, :] * u_prev + w[1:2, :] * u + w[2:3, :] * u_next
    zc = zc_ref[...].astype(f32)
    y_c = cb_ref[...].astype(f32) * conv * (zc * _sigmoid(zc))

    heads = []
    for hd in range(NH_A):
        cols = slice(hd * HD_A, (hd + 1) * HD_A)
        s = _nt_dot(qa_ref[:, cols], km_ref[:, cols]) * (HD_A ** -0.5)
        s = s - jnp.max(s, axis=-1, keepdims=True)
        e = jnp.exp(s)
        p = e / jnp.sum(e, axis=-1, keepdims=True)
        heads.append(jnp.dot(p.astype(bf16), vm_ref[:, cols], preferred_element_type=f32))
    za = za_ref[...].astype(f32)
    y_a = jnp.concatenate(heads, axis=1) * (za * _sigmoid(za))

    def proj(y, k):
        return jnp.dot(y.astype(bf16), wb_ref[k], preferred_element_type=f32)

    merged = (_sigmoid(mg0_ref[...].astype(f32)) * proj(ym_ref[...], 0)
              + _sigmoid(mg1_ref[...].astype(f32)) * proj(y_c, 1)
              + _sigmoid(mg2_ref[...].astype(f32)) * proj(y_a, 2))
    out_ref[...] = merged.astype(out_ref.dtype)


def _branches(p, ym, kv, convw, wb, *, batch, seq, tm):
    tm = min(tm, seq)
    nb = seq // tm
    hb = tm // HALO
    last_halo = batch * seq // HALO - 1

    def col(cb):
        return pl.BlockSpec((tm, D_MODEL), lambda b, i, cb=cb: (b * nb + i, cb))

    def halo_before(cb):
        return pl.BlockSpec((HALO, D_MODEL), lambda b, i, cb=cb: (jnp.maximum((b * nb + i) * hb - 1, 0), cb))

    def halo_after(cb):
        return pl.BlockSpec((HALO, D_MODEL),
                            lambda b, i, cb=cb: (jnp.minimum((b * nb + i + 1) * hb, last_halo), cb))

    n_mem = kv.shape[0] // batch
    const = lambda *shape: pl.BlockSpec(shape, lambda b, i: (0,) * len(shape), pipeline_mode=pl.Buffered(1))
    return pl.pallas_call(
        _branches_kernel,
        grid=(batch, nb),
        in_specs=[col(COL_CB), col(COL_CC), col(COL_CX), col(COL_ZC),
                  halo_before(COL_CC), halo_before(COL_CX), halo_after(COL_CC), halo_after(COL_CX),
                  col(COL_QA), col(COL_ZA), col(COL_MG), col(COL_MG + 1), col(COL_MG + 2),
                  pl.BlockSpec((tm, D_MODEL), lambda b, i: (b * nb + i, 0)),
                  pl.BlockSpec((n_mem, D_MODEL), lambda b, i: (b, 0)),
                  pl.BlockSpec((n_mem, D_MODEL), lambda b, i: (b, 1)),
                  const(3, D_MODEL), const(3, D_MODEL, D_MODEL)],
        out_specs=pl.BlockSpec((tm, D_MODEL), lambda b, i: (b * nb + i, 0)),
        out_shape=jax.ShapeDtypeStruct((batch * seq, D_MODEL), bf16),
        compiler_params=pltpu.CompilerParams(
            dimension_semantics=("arbitrary", "arbitrary"), vmem_limit_bytes=VMEM_LIMIT),
        name="branches",
    )(p, p, p, p, p, p, p, p, p, p, p, p, p, ym, kv, kv, convw, wb)


def _out_proj_kernel(x_ref, mrg_ref, w_ref, g_ref, y_ref):
    r = x_ref[...] + jnp.dot(mrg_ref[...], w_ref[...], preferred_element_type=f32)
    ms = jnp.mean(r * r, axis=-1, keepdims=True)
    y_ref[...] = r * lax.rsqrt(ms + EPS) * g_ref[...]


def _out_proj(x, mrg, w, g, *, tm):
    rows, d = x.shape
    tm = min(tm, rows)
    return pl.pallas_call(
        _out_proj_kernel,
        grid=(rows // tm,),
        in_specs=[pl.BlockSpec((tm, d), lambda i: (i, 0)),
                  pl.BlockSpec((tm, d), lambda i: (i, 0)),
                  pl.BlockSpec((d, d), lambda i: (0, 0)),
                  pl.BlockSpec((1, d), lambda i: (0, 0))],
        out_specs=pl.BlockSpec((tm, d), lambda i: (i, 0)),
        out_shape=jax.ShapeDtypeStruct((rows, d), f32),
        compiler_params=pltpu.CompilerParams(
            dimension_semantics=("arbitrary",), vmem_limit_bytes=VMEM_LIMIT),
        name="out_proj",
    )(x, mrg, w, g)


def _trunk(x, mem, wts):
    batch, seq, d = x.shape
    x2 = x.reshape(batch * seq, d)
    mem2 = mem.reshape(-1, d)
    p, gt = _norm_matmul(x2, wts["norm_g"], wts["w_main"], wts["w_gate_t"], wts["b_gate"], tm=1024, tn=1024)
    kv, = _norm_matmul(mem2, wts["mem_g"], wts["w_kv"], tm=1024, tn=1024)
    gt = gt.reshape(NH_M, GATE_ROWS, batch, seq // CHUNK, CHUNK)[:, :4].transpose(2, 0, 1, 3, 4)
    ym = _mlstm(p, gt, wts["mh_g"], batch=batch, seq=seq)
    mrg = _branches(p, ym, kv, wts["conv_w"], wts["w_branch"], batch=batch, seq=seq, tm=256)
    y = _out_proj(x2, mrg, wts["w_out"], wts["final_g"], tm=512)
    return y.reshape(batch, seq, d)


def kernel(x_prompt, x_sample, mem_prompt, mem_sample, norm_g, w_in, b_if, conv_w, mem_norm_g, w_kv_mem,
           mh_norm_g, w_branch, w_out, final_norm_g):
    assert norm_g.shape[0] == 1, "single-layer trunk"
    w = w_in[0]
    g0 = 5 * D_MODEL
    w_main = jnp.concatenate([w[:, :g0], w[:, g0 + N_GATE:]], axis=1).astype(bf16)
    wg = w[:, g0:g0 + N_GATE].reshape(D_MODEL, 4, NH_M).transpose(2, 1, 0)
    wg = jnp.pad(wg, ((0, 0), (0, GATE_ROWS - 4), (0, 0))).reshape(NH_M * GATE_ROWS, D_MODEL).astype(bf16)
    bg = jnp.pad(b_if[0].reshape(4, NH_M).T, ((0, 0), (0, GATE_ROWS - 4))).reshape(NH_M * GATE_ROWS, 1)
    wts = dict(
        norm_g=norm_g[0][None, :], w_main=w_main, w_gate_t=wg, b_gate=bg.astype(f32),
        mem_g=mem_norm_g[0][None, :], w_kv=w_kv_mem[0].astype(bf16), mh_g=mh_norm_g[0][None, :],
        conv_w=conv_w[0], w_branch=w_branch[0].astype(bf16), w_out=w_out[0].astype(bf16),
        final_g=final_norm_g[None, :])
    return (_trunk(x_prompt, mem_prompt, wts), _trunk(x_sample, mem_sample, wts))
```

```python
import functools
import math

import jax
import jax.numpy as jnp
from jax import lax
from jax.experimental import pallas as pl
from jax.experimental.pallas import tpu as pltpu

D_MODEL = 2048
NH_M = 8
HD_M = D_MODEL // NH_M
CHUNK = 128
NH_A = 4
HD_A = D_MODEL // NH_A
N_GATE = 4 * NH_M
GATE_ROWS = 8
EPS = 1e-6
COL_QM, COL_KM, COL_VM, COL_OM, COL_ZM, COL_CB, COL_CC, COL_CX, COL_ZC, COL_QA, COL_ZA, COL_MG = range(12)
N_MAIN = 14 * D_MODEL
HALO = 16
VMEM_LIMIT = 56 * 1024 * 1024

f32 = jnp.float32
bf16 = jnp.bfloat16


def _sigmoid(x):
    return 0.5 * jnp.tanh(0.5 * x) + 0.5


def _nt_dot(a, b):
    return lax.dot_general(a, b, (((1,), (1,)), ((), ())), preferred_element_type=f32)


def _tn_dot(a, b):
    return lax.dot_general(a, b, (((0,), (0,)), ((), ())), preferred_element_type=f32)


def _norm_matmul_kernel(x_ref, g_ref, w_ref, *rest, with_gates):
    if with_gates:
        wgt_ref, bif_ref, p_ref, gt_ref, h_scr = rest
    else:
        p_ref, h_scr = rest

    @pl.when(pl.program_id(1) == 0)
    def _():
        xf = x_ref[...]
        ms = jnp.mean(xf * xf, axis=-1, keepdims=True)
        h = (xf * lax.rsqrt(ms + EPS) * g_ref[...]).astype(bf16)
        h_scr[...] = h
        if with_gates:
            gt_ref[...] = _nt_dot(wgt_ref[...], h) + bif_ref[...]

    p_ref[...] = jnp.dot(h_scr[...], w_ref[...], preferred_element_type=f32).astype(p_ref.dtype)


def _norm_matmul(x, g, w, wgt=None, bif=None, *, tm, tn):
    rows, d = x.shape
    n = w.shape[1]
    tm = min(tm, rows)
    tn = min(tn, n)
    with_gates = wgt is not None
    in_specs = [
        pl.BlockSpec((tm, d), lambda i, j: (i, 0)),
        pl.BlockSpec((1, d), lambda i, j: (0, 0)),
        pl.BlockSpec((d, tn), lambda i, j: (0, j)),
    ]
    out_shape = [jax.ShapeDtypeStruct((rows, n), bf16)]
    out_specs = [pl.BlockSpec((tm, tn), lambda i, j: (i, j))]
    args = [x, g, w]
    if with_gates:
        ng = wgt.shape[0]
        in_specs += [pl.BlockSpec((ng, d), lambda i, j: (0, 0)), pl.BlockSpec((ng, 1), lambda i, j: (0, 0))]
        out_shape.append(jax.ShapeDtypeStruct((ng, rows), f32))
        out_specs.append(pl.BlockSpec((ng, tm), lambda i, j: (0, i)))
        args += [wgt, bif]
    return pl.pallas_call(
        functools.partial(_norm_matmul_kernel, with_gates=with_gates),
        grid=(rows // tm, n // tn),
        in_specs=in_specs,
        out_specs=out_specs,
        out_shape=out_shape,
        scratch_shapes=[pltpu.VMEM((tm, d), bf16)],
        compiler_params=pltpu.CompilerParams(
            dimension_semantics=("arbitrary", "arbitrary"), vmem_limit_bytes=VMEM_LIMIT),
        name="norm_matmul_gates" if with_gates else "norm_matmul",
    )(*args)


def _lane_scan(x, op, reverse):
    lane = lax.broadcasted_iota(jnp.int32, x.shape, 1)
    n = x.shape[1]
    sh = 1
    while sh < n:
        if reverse:
            x = jnp.where(lane < n - sh, op(x, pltpu.roll(x, n - sh, axis=1)), x)
        else:
            x = jnp.where(lane >= sh, op(x, pltpu.roll(x, sh, axis=1)), x)
        sh *= 2
    return x


def _col_bcast(row):
    L = row.shape[1]
    return jnp.broadcast_to(row, (L, L)).T


def _mlstm_kernel(q_ref, k_ref, v_ref, o_ref, z_ref, gt_ref, mhg_ref, y_ref,
                  w_scr, b_scr, cw_scr, tot_scr, mw_scr, h_scr, c_scr, n_scr):
    S, d = q_ref.shape
    L = CHUNK
    nc = S // L
    log_k_scale = -0.5 * math.log(HD_M)

    i_f, i_b, f_f, f_b = gt_ref[0], gt_ref[1], gt_ref[2], gt_ref[3]
    for dr, (i_pre, f_pre) in enumerate(((i_f, f_f), (i_b, f_b))):
        rev = dr == 1
        lf = jnp.minimum(f_pre, 0.0) - jnp.log1p(jnp.exp(-jnp.abs(f_pre)))
        b = _lane_scan(lf, jnp.add, rev)
        tot = b + _lane_scan(lf, jnp.add, not rev) - lf
        w = i_pre - b
        cw = _lane_scan(w, jnp.maximum, rev)
        mw = jnp.maximum(cw, _lane_scan(w, jnp.maximum, not rev))
        w_scr[dr] = w
        b_scr[dr] = b
        cw_scr[dr] = cw
        tot_scr[dr] = tot
        mw_scr[dr] = mw

    r = lax.broadcasted_iota(jnp.int32, (L, L), 0)
    c = lax.broadcasted_iota(jnp.int32, (L, L), 1)
    c_scr[...] = jnp.zeros_like(c_scr)
    n_scr[...] = jnp.zeros_like(n_scr)

    def chunk(ci, dr, m):
        causal = (c >= r) if dr == 1 else (c <= r)
        rows = pl.ds(pl.multiple_of(ci * L, L), L)
        one = pl.ds(ci, 1)
        w_row, tot, mw = w_scr[dr, one, :], tot_scr[dr, one, :], mw_scr[dr, one, :]
        qc, kc, vc = q_ref[rows, :], k_ref[rows, :], v_ref[rows, :]
        n8 = n_scr[dr]
        kn = jnp.concatenate([kc, jnp.broadcast_to(n8[0:1, :].astype(bf16), (L, d))], axis=0)
        qkn = _nt_dot(qc, kn)
        g_row = jnp.maximum(cw_scr[dr, one, :], m)
        g = _col_bcast(g_row)
        a = _col_bcast(jnp.exp(m - g_row))
        exp_neg_mt = _col_bcast(jnp.exp(-(g_row + b_scr[dr, one, :])))
        w_row = w_row + log_k_scale
        s = qkn[:, :L] * jnp.exp(jnp.where(causal, w_row - g, -jnp.inf))
        cx = c_scr[dr]
        hx = (jnp.dot(s.astype(bf16), vc, preferred_element_type=f32)
              + jnp.concatenate([a, a], axis=1) * jnp.dot(qc, cx.astype(bf16), preferred_element_type=f32))
        den = jnp.maximum(jnp.abs(jnp.sum(s, axis=1, keepdims=True) + a * qkn[:, L:]), exp_neg_mt)
        rden = 1.0 / den
        h_scr[dr, rows, :] = hx * jnp.concatenate([rden, rden], axis=1)

        m_up = jnp.maximum(m, mw)
        m_new = tot + m_up
        wk = jnp.exp(tot + w_row - m_new)
        kw = kc.T * wk.astype(bf16)
        decay = jnp.exp(m - m_up)
        decay = jnp.concatenate([decay, decay], axis=1)
        c_scr[dr] = decay * cx + jnp.dot(kw, vc, preferred_element_type=f32)
        n_scr[dr] = decay * n8 + jnp.dot(jnp.broadcast_to(wk, (8, L)).astype(bf16), kc,
                                         preferred_element_type=f32)
        return m_new

    quarter_gain = 0.25 * mhg_ref[...]

    def finish(ci):
        rows = pl.ds(pl.multiple_of(ci * L, L), L)
        h = h_scr[0, rows, :] + h_scr[1, rows, :]
        mu = jnp.mean(h, axis=-1, keepdims=True)
        hc = h - mu
        var = jnp.mean(hc * hc, axis=-1, keepdims=True)
        hn = hc * lax.rsqrt(var + EPS) * quarter_gain
        o = o_ref[rows, :].astype(f32)
        z = z_ref[rows, :].astype(f32)
        gate = (1.0 + jnp.tanh(0.5 * o)) * ((1.0 + jnp.tanh(0.5 * z)) * z)
        y_ref[rows, :] = (hn * gate).astype(y_ref.dtype)

    def both(it, ms):
        return chunk(it, 0, ms[0]), chunk(nc - 1 - it, 1, ms[1])

    def both_and_finish(it, ms):
        ms = both(it, ms)
        finish(it)
        finish(nc - 1 - it)
        return ms

    def unroll(trips, most):
        return max(u for u in (1, 2, 4) if u <= most and trips % u == 0)

    half = nc // 2
    m0 = jnp.zeros((1, L), f32)
    ms = lax.fori_loop(0, half, both, (m0, m0), unroll=unroll(half, 4)) if half else (m0, m0)
    lax.fori_loop(half, nc, both_and_finish, ms, unroll=unroll(nc - half, 2))


def _mlstm(p, gt, mhg, *, batch, seq):
    nc = seq // CHUNK

    def col(cb):
        return pl.BlockSpec((seq, HD_M), lambda b, h, cb=cb: (b, cb * NH_M + h))

    gate_rows = pltpu.VMEM((2, nc, CHUNK), f32)
    return pl.pallas_call(
        _mlstm_kernel,
        grid=(batch, NH_M),
        in_specs=[col(COL_QM), col(COL_KM), col(COL_VM), col(COL_OM), col(COL_ZM),
                  pl.BlockSpec((None, None, 4, nc, CHUNK), lambda b, h: (b, h, 0, 0, 0)),
                  pl.BlockSpec((1, HD_M), lambda b, h: (0, h))],
        out_specs=pl.BlockSpec((seq, HD_M), lambda b, h: (b, h)),
        out_shape=jax.ShapeDtypeStruct((batch * seq, D_MODEL), bf16),
        scratch_shapes=[gate_rows, gate_rows, gate_rows, gate_rows, gate_rows,
                        pltpu.VMEM((2, seq, HD_M), f32), pltpu.VMEM((2, HD_M, HD_M), f32),
                        pltpu.VMEM((2, 8, HD_M), f32)],
        compiler_params=pltpu.CompilerParams(
            dimension_semantics=("arbitrary", "arbitrary"), vmem_limit_bytes=VMEM_LIMIT),
        name="mlstm",
    )(p, p, p, p, p, gt, mhg)


def _branches_kernel(cb_ref, cc_ref, cx_ref, zc_ref, ccp_ref, cxp_ref, ccn_ref, cxn_ref,
                     qa_ref, za_ref, mg0_ref, mg1_ref, mg2_ref, ym_ref, km_ref, vm_ref,
                     convw_ref, wb_ref, out_ref):
    tm = cb_ref.shape[0]
    i = pl.program_id(1)
    ni = pl.num_programs(1)

    u = cc_ref[...].astype(f32) * cx_ref[...].astype(f32)
    u_before = (ccp_ref[HALO - 1:HALO, :].astype(f32) * cxp_ref[HALO - 1:HALO, :].astype(f32)
                * (i > 0).astype(f32))
    u_after = ccn_ref[0:1, :].astype(f32) * cxn_ref[0:1, :].astype(f32) * (i < ni - 1).astype(f32)
    row = lax.broadcasted_iota(jnp.int32, (tm, 1), 0)
    u_prev = jnp.where(row == 0, u_before, pltpu.roll(u, 1, axis=0))
    u_next = jnp.where(row == tm - 1, u_after, pltpu.roll(u, tm - 1, axis=0))
    w = convw_ref[...]
    conv = w[0:1, :] * u_prev + w[1:2, :] * u + w[2:3, :] * u_next
    zc = zc_ref[...].astype(f32)
    y_c = cb_ref[...].astype(f32) * conv * (zc * _sigmoid(zc))

    heads = []
    for hd in range(NH_A):
        cols = slice(hd * HD_A, (hd + 1) * HD_A)
        s = _nt_dot(qa_ref[:, cols], km_ref[:, cols]) * (HD_A ** -0.5)
        s = s - jnp.max(s, axis=-1, keepdims=True)
        e = jnp.exp(s)
        p = e / jnp.sum(e, axis=-1, keepdims=True)
        heads.append(jnp.dot(p.astype(bf16), vm_ref[:, cols], preferred_element_type=f32))
    za = za_ref[...].astype(f32)
    y_a = jnp.concatenate(heads, axis=1) * (za * _sigmoid(za))

    def proj(y, k):
        return jnp.dot(y.astype(bf16), wb_ref[k], preferred_element_type=f32)

    merged = (_sigmoid(mg0_ref[...].astype(f32)) * proj(ym_ref[...], 0)
              + _sigmoid(mg1_ref[...].astype(f32)) * proj(y_c, 1)
              + _sigmoid(mg2_ref[...].astype(f32)) * proj(y_a, 2))
    out_ref[...] = merged.astype(out_ref.dtype)


def _branches(p, ym, kv, convw, wb, *, batch, seq, tm):
    tm = min(tm, seq)
    nb = seq // tm
    hb = tm // HALO
    last_halo = batch * seq // HALO - 1

    def col(cb):
        return pl.BlockSpec((tm, D_MODEL), lambda b, i, cb=cb: (b * nb + i, cb))

    def halo_before(cb):
        return pl.BlockSpec((HALO, D_MODEL), lambda b, i, cb=cb: (jnp.maximum((b * nb + i) * hb - 1, 0), cb))

    def halo_after(cb):
        return pl.BlockSpec((HALO, D_MODEL),
                            lambda b, i, cb=cb: (jnp.minimum((b * nb + i + 1) * hb, last_halo), cb))

    n_mem = kv.shape[0] // batch
    const = lambda *shape: pl.BlockSpec(shape, lambda b, i: (0,) * len(shape), pipeline_mode=pl.Buffered(1))
    return pl.pallas_call(
        _branches_kernel,
        grid=(batch, nb),
        in_specs=[col(COL_CB), col(COL_CC), col(COL_CX), col(COL_ZC),
                  halo_before(COL_CC), halo_before(COL_CX), halo_after(COL_CC), halo_after(COL_CX),
                  col(COL_QA), col(COL_ZA), col(COL_MG), col(COL_MG + 1), col(COL_MG + 2),
                  pl.BlockSpec((tm, D_MODEL), lambda b, i: (b * nb + i, 0)),
                  pl.BlockSpec((n_mem, D_MODEL), lambda b, i: (b, 0)),
                  pl.BlockSpec((n_mem, D_MODEL), lambda b, i: (b, 1)),
                  const(3, D_MODEL), const(3, D_MODEL, D_MODEL)],
        out_specs=pl.BlockSpec((tm, D_MODEL), lambda b, i: (b * nb + i, 0)),
        out_shape=jax.ShapeDtypeStruct((batch * seq, D_MODEL), bf16),
        compiler_params=pltpu.CompilerParams(
            dimension_semantics=("arbitrary", "arbitrary"), vmem_limit_bytes=VMEM_LIMIT),
        name="branches",
    )(p, p, p, p, p, p, p, p, p, p, p, p, p, ym, kv, kv, convw, wb)


def _out_proj_kernel(x_ref, mrg_ref, w_ref, g_ref, y_ref):
    r = x_ref[...] + jnp.dot(mrg_ref[...], w_ref[...], preferred_element_type=f32)
    ms = jnp.mean(r * r, axis=-1, keepdims=True)
    y_ref[...] = r * lax.rsqrt(ms + EPS) * g_ref[...]


def _out_proj(x, mrg, w, g, *, tm):
    rows, d = x.shape
    tm = min(tm, rows)
    return pl.pallas_call(
        _out_proj_kernel,
        grid=(rows // tm,),
        in_specs=[pl.BlockSpec((tm, d), lambda i: (i, 0)),
                  pl.BlockSpec((tm, d), lambda i: (i, 0)),
                  pl.BlockSpec((d, d), lambda i: (0, 0)),
                  pl.BlockSpec((1, d), lambda i: (0, 0))],
        out_specs=pl.BlockSpec((tm, d), lambda i: (i, 0)),
        out_shape=jax.ShapeDtypeStruct((rows, d), f32),
        compiler_params=pltpu.CompilerParams(
            dimension_semantics=("arbitrary",), vmem_limit_bytes=VMEM_LIMIT),
        name="out_proj",
    )(x, mrg, w, g)


def _trunk(x, mem, wts):
    batch, seq, d = x.shape
    x2 = x.reshape(batch * seq, d)
    mem2 = mem.reshape(-1, d)
    p, gt = _norm_matmul(x2, wts["norm_g"], wts["w_main"], wts["w_gate_t"], wts["b_gate"], tm=1024, tn=2048)
    kv, = _norm_matmul(mem2, wts["mem_g"], wts["w_kv"], tm=1024, tn=1024)
    gt = gt.reshape(NH_M, GATE_ROWS, batch, seq // CHUNK, CHUNK)[:, :4].transpose(2, 0, 1, 3, 4)
    ym = _mlstm(p, gt, wts["mh_g"], batch=batch, seq=seq)
    mrg = _branches(p, ym, kv, wts["conv_w"], wts["w_branch"], batch=batch, seq=seq, tm=256)
    y = _out_proj(x2, mrg, wts["w_out"], wts["final_g"], tm=512)
    return y.reshape(batch, seq, d)


def kernel(x_prompt, x_sample, mem_prompt, mem_sample, norm_g, w_in, b_if, conv_w, mem_norm_g, w_kv_mem,
           mh_norm_g, w_branch, w_out, final_norm_g):
    assert norm_g.shape[0] == 1, "single-layer trunk"
    w = w_in[0]
    g0 = 5 * D_MODEL
    w_main = jnp.concatenate([w[:, :g0], w[:, g0 + N_GATE:]], axis=1).astype(bf16)
    wg = w[:, g0:g0 + N_GATE].reshape(D_MODEL, 4, NH_M).transpose(2, 1, 0)
    wg = jnp.pad(wg, ((0, 0), (0, GATE_ROWS - 4), (0, 0))).reshape(NH_M * GATE_ROWS, D_MODEL).astype(bf16)
    bg = jnp.pad(b_if[0].reshape(4, NH_M).T, ((0, 0), (0, GATE_ROWS - 4))).reshape(NH_M * GATE_ROWS, 1)
    wts = dict(
        norm_g=norm_g[0][None, :], w_main=w_main, w_gate_t=wg, b_gate=bg.astype(f32),
        mem_g=mem_norm_g[0][None, :], w_kv=w_kv_mem[0].astype(bf16), mh_g=mh_norm_g[0][None, :],
        conv_w=conv_w[0], w_branch=w_branch[0].astype(bf16), w_out=w_out[0].astype(bf16),
        final_g=final_norm_g[None, :])
    return (_trunk(x_prompt, mem_prompt, wts), _trunk(x_sample, mem_sample, wts))
```

```python
import functools
import math

import jax
import jax.numpy as jnp
from jax import lax
from jax.experimental import pallas as pl
from jax.experimental.pallas import tpu as pltpu

D_MODEL = 2048
NH_M = 8
HD_M = D_MODEL // NH_M
CHUNK = 128
NH_A = 4
HD_A = D_MODEL // NH_A
N_GATE = 4 * NH_M
GATE_ROWS = 8
EPS = 1e-6
COL_QM, COL_KM, COL_VM, COL_OM, COL_ZM, COL_CB, COL_CC, COL_CX, COL_ZC, COL_QA, COL_ZA, COL_MG = range(12)
N_MAIN = 14 * D_MODEL
HALO = 16
VMEM_LIMIT = 56 * 1024 * 1024

f32 = jnp.float32
bf16 = jnp.bfloat16


def _sigmoid(x):
    return 0.5 * jnp.tanh(0.5 * x) + 0.5


def _nt_dot(a, b):
    return lax.dot_general(a, b, (((1,), (1,)), ((), ())), preferred_element_type=f32)


def _tn_dot(a, b):
    return lax.dot_general(a, b, (((0,), (0,)), ((), ())), preferred_element_type=f32)


def _norm_matmul_kernel(x_ref, g_ref, w_ref, *rest, with_gates):
    if with_gates:
        wgt_ref, bif_ref, p_ref, gt_ref, h_scr = rest
    else:
        p_ref, h_scr = rest

    @pl.when(pl.program_id(1) == 0)
    def _():
        xf = x_ref[...]
        ms = jnp.mean(xf * xf, axis=-1, keepdims=True)
        h = (xf * lax.rsqrt(ms + EPS) * g_ref[...]).astype(bf16)
        h_scr[...] = h
        if with_gates:
            gt_ref[...] = _nt_dot(wgt_ref[...], h) + bif_ref[...]

    p_ref[...] = jnp.dot(h_scr[...], w_ref[...], preferred_element_type=f32).astype(p_ref.dtype)


def _norm_matmul(x, g, w, wgt=None, bif=None, *, tm, tn):
    rows, d = x.shape
    n = w.shape[1]
    tm = min(tm, rows)
    tn = min(tn, n)
    assert rows % tm == 0 and n % tn == 0, (rows, n, tm, tn)
    with_gates = wgt is not None
    in_specs = [
        pl.BlockSpec((tm, d), lambda i, j: (i, 0)),
        pl.BlockSpec((1, d), lambda i, j: (0, 0)),
        pl.BlockSpec((d, tn), lambda i, j: (0, j)),
    ]
    out_shape = [jax.ShapeDtypeStruct((rows, n), bf16)]
    out_specs = [pl.BlockSpec((tm, tn), lambda i, j: (i, j))]
    args = [x, g, w]
    if with_gates:
        ng = wgt.shape[0]
        in_specs += [pl.BlockSpec((ng, d), lambda i, j: (0, 0)), pl.BlockSpec((ng, 1), lambda i, j: (0, 0))]
        out_shape.append(jax.ShapeDtypeStruct((ng, rows), f32))
        out_specs.append(pl.BlockSpec((ng, tm), lambda i, j: (0, i)))
        args += [wgt, bif]
    return pl.pallas_call(
        functools.partial(_norm_matmul_kernel, with_gates=with_gates),
        grid=(rows // tm, n // tn),
        in_specs=in_specs,
        out_specs=out_specs,
        out_shape=out_shape,
        scratch_shapes=[pltpu.VMEM((tm, d), bf16)],
        compiler_params=pltpu.CompilerParams(
            dimension_semantics=("arbitrary", "arbitrary"), vmem_limit_bytes=VMEM_LIMIT),
        name="norm_matmul_gates" if with_gates else "norm_matmul",
    )(*args)


def _lane_scan(x, op, reverse):
    lane = lax.broadcasted_iota(jnp.int32, x.shape, 1)
    n = x.shape[1]
    sh = 1
    while sh < n:
        if reverse:
            x = jnp.where(lane < n - sh, op(x, pltpu.roll(x, n - sh, axis=1)), x)
        else:
            x = jnp.where(lane >= sh, op(x, pltpu.roll(x, sh, axis=1)), x)
        sh *= 2
    return x


def _col_bcast(row):
    L = row.shape[1]
    return jnp.broadcast_to(row, (L, L)).T


GS_W, GS_G, GS_A, GS_ENM, GS_WK, GS_DECAY = range(6)


def _gate_scan_kernel(gt_ref, out_ref, *, nc):
    _, R, L = gt_ref.shape
    log_k_scale = -0.5 * math.log(HD_M)
    ck = lax.broadcasted_iota(jnp.int32, (R, L), 0) % nc
    for dr in range(2):
        rev = dr == 1
        i_pre, f_pre = gt_ref[dr], gt_ref[2 + dr]
        lf = jnp.minimum(f_pre, 0.0) - jnp.log1p(jnp.exp(-jnp.abs(f_pre)))
        b = _lane_scan(lf, jnp.add, rev)
        tot = b + _lane_scan(lf, jnp.add, not rev) - lf
        w = i_pre - b
        cw = _lane_scan(w, jnp.maximum, rev)
        mw = jnp.maximum(cw, _lane_scan(w, jnp.maximum, not rev))

        a, c = tot, mw + tot
        sh = 1
        while sh < nc:
            shift, valid = (R - sh, ck < nc - sh) if rev else (sh, ck >= sh)
            c = jnp.where(valid, jnp.maximum(pltpu.roll(c, shift, axis=0) + a, c), c)
            a = jnp.where(valid, pltpu.roll(a, shift, axis=0) + a, a)
            sh *= 2
        m_after = jnp.maximum(a, c)
        shift, valid = (R - 1, ck < nc - 1) if rev else (1, ck >= 1)
        m = jnp.where(valid, pltpu.roll(m_after, shift, axis=0), 0.0)

        g = jnp.maximum(cw, m)
        m_up = jnp.maximum(m, mw)
        w = w + log_k_scale
        out_ref[dr, GS_W] = w
        out_ref[dr, GS_G] = g
        out_ref[dr, GS_A] = jnp.exp(m - g)
        out_ref[dr, GS_ENM] = jnp.exp(-(g + b))
        out_ref[dr, GS_WK] = jnp.exp(w - m_up)
        out_ref[dr, GS_DECAY] = jnp.exp(m - m_up)


def _gate_scan(gt, *, nc):
    _, rows, L = gt.shape
    rb = min(rows, 256)
    assert rows % rb == 0 and rb % nc == 0 and rb % 8 == 0, (rows, nc)
    return pl.pallas_call(
        functools.partial(_gate_scan_kernel, nc=nc),
        grid=(rows // rb,),
        in_specs=[pl.BlockSpec((4, rb, L), lambda i: (0, i, 0))],
        out_specs=pl.BlockSpec((2, 6, rb, L), lambda i: (0, 0, i, 0)),
        out_shape=jax.ShapeDtypeStruct((2, 6, rows, L), f32),
        compiler_params=pltpu.CompilerParams(dimension_semantics=("arbitrary",)),
        name="gate_scan",
    )(gt)


def _mlstm_kernel(q_ref, k_ref, v_ref, o_ref, z_ref, gs_ref, mhg_ref, y_ref, h_scr, c_scr, n_scr):
    S, d = q_ref.shape
    L = CHUNK
    nc = S // L

    r = lax.broadcasted_iota(jnp.int32, (L, L), 0)
    c = lax.broadcasted_iota(jnp.int32, (L, L), 1)
    c_scr[...] = jnp.zeros_like(c_scr)
    n_scr[...] = jnp.zeros_like(n_scr)

    def chunk(ci, dr):
        causal = (c >= r) if dr == 1 else (c <= r)
        rows = pl.ds(pl.multiple_of(ci * L, L), L)
        one = pl.ds(ci, 1)
        qc, kc, vc = q_ref[rows, :], k_ref[rows, :], v_ref[rows, :]
        n8 = n_scr[dr]
        kn = jnp.concatenate([kc, jnp.broadcast_to(n8[0:1, :].astype(bf16), (L, d))], axis=0)
        qkn = _nt_dot(qc, kn)
        g = _col_bcast(gs_ref[dr, GS_G, one, :])
        a = _col_bcast(gs_ref[dr, GS_A, one, :])
        exp_neg_mt = _col_bcast(gs_ref[dr, GS_ENM, one, :])
        s = qkn[:, :L] * jnp.exp(jnp.where(causal, gs_ref[dr, GS_W, one, :] - g, -jnp.inf))
        cx = c_scr[dr]
        hx = (jnp.dot(s.astype(bf16), vc, preferred_element_type=f32)
              + jnp.concatenate([a, a], axis=1) * jnp.dot(qc, cx.astype(bf16), preferred_element_type=f32))
        den = jnp.maximum(jnp.abs(jnp.sum(s, axis=1, keepdims=True) + a * qkn[:, L:]), exp_neg_mt)
        rden = 1.0 / den
        h_scr[dr, rows, :] = hx * jnp.concatenate([rden, rden], axis=1)

        wk = gs_ref[dr, GS_WK, one, :]
        kw = kc.T * wk.astype(bf16)
        decay = gs_ref[dr, GS_DECAY, one, :]
        decay = jnp.concatenate([decay, decay], axis=1)
        c_scr[dr] = decay * cx + jnp.dot(kw, vc, preferred_element_type=f32)
        n_scr[dr] = decay * n8 + jnp.dot(jnp.broadcast_to(wk, (8, L)).astype(bf16), kc,
                                         preferred_element_type=f32)

    quarter_gain = 0.25 * mhg_ref[...]

    def finish(ci):
        rows = pl.ds(pl.multiple_of(ci * L, L), L)
        h = h_scr[0, rows, :] + h_scr[1, rows, :]
        mu = jnp.mean(h, axis=-1, keepdims=True)
        hc = h - mu
        var = jnp.mean(hc * hc, axis=-1, keepdims=True)
        hn = hc * lax.rsqrt(var + EPS) * quarter_gain
        o = o_ref[rows, :].astype(f32)
        z = z_ref[rows, :].astype(f32)
        gate = (1.0 + jnp.tanh(0.5 * o)) * ((1.0 + jnp.tanh(0.5 * z)) * z)
        y_ref[rows, :] = (hn * gate).astype(y_ref.dtype)

    def both(it, carry):
        chunk(it, 0)
        chunk(nc - 1 - it, 1)
        return carry

    def both_and_finish(it, carry):
        both(it, carry)
        finish(it)
        finish(nc - 1 - it)
        return carry

    def unroll(trips, most):
        return max(u for u in (1, 2, 4) if u <= most and trips % u == 0)

    half = nc // 2
    if half:
        lax.fori_loop(0, half, both, 0, unroll=unroll(half, 4))
    lax.fori_loop(half, nc, both_and_finish, 0, unroll=unroll(nc - half, 2))


def _mlstm(p, gs, mhg, *, batch, seq):
    nc = seq // CHUNK
    assert seq % CHUNK == 0 and nc % 8 == 0, seq

    def col(cb):
        return pl.BlockSpec((seq, HD_M), lambda b, h, cb=cb: (b, cb * NH_M + h))

    return pl.pallas_call(
        _mlstm_kernel,
        grid=(batch, NH_M),
        in_specs=[col(COL_QM), col(COL_KM), col(COL_VM), col(COL_OM), col(COL_ZM),
                  pl.BlockSpec((2, 6, nc, CHUNK), lambda b, h: (0, 0, b * NH_M + h, 0)),
                  pl.BlockSpec((1, HD_M), lambda b, h: (0, h))],
        out_specs=pl.BlockSpec((seq, HD_M), lambda b, h: (b, h)),
        out_shape=jax.ShapeDtypeStruct((batch * seq, D_MODEL), bf16),
        scratch_shapes=[pltpu.VMEM((2, seq, HD_M), f32), pltpu.VMEM((2, HD_M, HD_M), f32),
                        pltpu.VMEM((2, 8, HD_M), f32)],
        compiler_params=pltpu.CompilerParams(
            dimension_semantics=("arbitrary", "arbitrary"), vmem_limit_bytes=VMEM_LIMIT),
        name="mlstm",
    )(p, p, p, p, p, gs, mhg)


def _branches_kernel(cb_ref, cc_ref, cx_ref, zc_ref, ccp_ref, cxp_ref, ccn_ref, cxn_ref,
                     qa_ref, za_ref, mg0_ref, mg1_ref, mg2_ref, ym_ref, km_ref, vm_ref,
                     convw_ref, wb_ref, out_ref):
    tm = cb_ref.shape[0]
    i = pl.program_id(1)
    ni = pl.num_programs(1)

    u = cc_ref[...].astype(f32) * cx_ref[...].astype(f32)
    u_before = (ccp_ref[HALO - 1:HALO, :].astype(f32) * cxp_ref[HALO - 1:HALO, :].astype(f32)
                * (i > 0).astype(f32))
    u_after = ccn_ref[0:1, :].astype(f32) * cxn_ref[0:1, :].astype(f32) * (i < ni - 1).astype(f32)
    row = lax.broadcasted_iota(jnp.int32, (tm, 1), 0)
    u_prev = jnp.where(row == 0, u_before, pltpu.roll(u, 1, axis=0))
    u_next = jnp.where(row == tm - 1, u_after, pltpu.roll(u, tm - 1, axis=0))
    w = convw_ref[...]
    conv = w[0:1, :] * u_prev + w[1:2, :] * u + w[2:3, :] * u_next
    zc = zc_ref[...].astype(f32)
    y_c = cb_ref[...].astype(f32) * conv * (zc * _sigmoid(zc))

    heads = []
    for hd in range(NH_A):
        cols = slice(hd * HD_A, (hd + 1) * HD_A)
        s = _nt_dot(qa_ref[:, cols], km_ref[:, cols]) * (HD_A ** -0.5)
        s = s - jnp.max(s, axis=-1, keepdims=True)
        e = jnp.exp(s)
        p = e / jnp.sum(e, axis=-1, keepdims=True)
        heads.append(jnp.dot(p.astype(bf16), vm_ref[:, cols], preferred_element_type=f32))
    za = za_ref[...].astype(f32)
    y_a = jnp.concatenate(heads, axis=1) * (za * _sigmoid(za))

    def proj(y, k):
        return jnp.dot(y.astype(bf16), wb_ref[k], preferred_element_type=f32)

    merged = (_sigmoid(mg0_ref[...].astype(f32)) * proj(ym_ref[...], 0)
              + _sigmoid(mg1_ref[...].astype(f32)) * proj(y_c, 1)
              + _sigmoid(mg2_ref[...].astype(f32)) * proj(y_a, 2))
    out_ref[...] = merged.astype(out_ref.dtype)


def _branches(p, ym, kv, convw, wb, *, batch, seq, tm):
    tm = min(tm, seq)
    assert seq % tm == 0 and tm % HALO == 0, (seq, tm)
    nb = seq // tm
    hb = tm // HALO
    last_halo = batch * seq // HALO - 1

    def col(cb):
        return pl.BlockSpec((tm, D_MODEL), lambda b, i, cb=cb: (b * nb + i, cb))

    def halo_before(cb):
        return pl.BlockSpec((HALO, D_MODEL), lambda b, i, cb=cb: (jnp.maximum((b * nb + i) * hb - 1, 0), cb))

    def halo_after(cb):
        return pl.BlockSpec((HALO, D_MODEL),
                            lambda b, i, cb=cb: (jnp.minimum((b * nb + i + 1) * hb, last_halo), cb))

    n_mem = kv.shape[0] // batch
    const = lambda *shape: pl.BlockSpec(shape, lambda b, i: (0,) * len(shape), pipeline_mode=pl.Buffered(1))
    return pl.pallas_call(
        _branches_kernel,
        grid=(batch, nb),
        in_specs=[col(COL_CB), col(COL_CC), col(COL_CX), col(COL_ZC),
                  halo_before(COL_CC), halo_before(COL_CX), halo_after(COL_CC), halo_after(COL_CX),
                  col(COL_QA), col(COL_ZA), col(COL_MG), col(COL_MG + 1), col(COL_MG + 2),
                  pl.BlockSpec((tm, D_MODEL), lambda b, i: (b * nb + i, 0)),
                  pl.BlockSpec((n_mem, D_MODEL), lambda b, i: (b, 0)),
                  pl.BlockSpec((n_mem, D_MODEL), lambda b, i: (b, 1)),
                  const(3, D_MODEL), const(3, D_MODEL, D_MODEL)],
        out_specs=pl.BlockSpec((tm, D_MODEL), lambda b, i: (b * nb + i, 0)),
        out_shape=jax.ShapeDtypeStruct((batch * seq, D_MODEL), bf16),
        compiler_params=pltpu.CompilerParams(
            dimension_semantics=("arbitrary", "arbitrary"), vmem_limit_bytes=VMEM_LIMIT),
        name="branches",
    )(p, p, p, p, p, p, p, p, p, p, p, p, p, ym, kv, kv, convw, wb)


def _out_proj_kernel(x_ref, mrg_ref, w_ref, g_ref, y_ref):
    r = x_ref[...] + jnp.dot(mrg_ref[...], w_ref[...], preferred_element_type=f32)
    ms = jnp.mean(r * r, axis=-1, keepdims=True)
    y_ref[...] = r * lax.rsqrt(ms + EPS) * g_ref[...]


def _out_proj(x, mrg, w, g, *, tm):
    rows, d = x.shape
    tm = min(tm, rows)
    assert rows % tm == 0, (rows, tm)
    return pl.pallas_call(
        _out_proj_kernel,
        grid=(rows // tm,),
        in_specs=[pl.BlockSpec((tm, d), lambda i: (i, 0)),
                  pl.BlockSpec((tm, d), lambda i: (i, 0)),
                  pl.BlockSpec((d, d), lambda i: (0, 0)),
                  pl.BlockSpec((1, d), lambda i: (0, 0))],
        out_specs=pl.BlockSpec((tm, d), lambda i: (i, 0)),
        out_shape=jax.ShapeDtypeStruct((rows, d), f32),
        compiler_params=pltpu.CompilerParams(
            dimension_semantics=("arbitrary",), vmem_limit_bytes=VMEM_LIMIT),
        name="out_proj",
    )(x, mrg, w, g)


def _trunk(x, mem, wts):
    batch, seq, d = x.shape
    x2 = x.reshape(batch * seq, d)
    mem2 = mem.reshape(-1, d)
    p, gt = _norm_matmul(x2, wts["norm_g"], wts["w_main"], wts["w_gate_t"], wts["b_gate"], tm=1024, tn=2048)
    kv, = _norm_matmul(mem2, wts["mem_g"], wts["w_kv"], tm=1024, tn=1024)
    nc = seq // CHUNK
    gt = gt.reshape(NH_M, GATE_ROWS, batch, nc, CHUNK)[:, :4].transpose(1, 2, 0, 3, 4)
    gs = _gate_scan(gt.reshape(4, batch * NH_M * nc, CHUNK), nc=nc)
    ym = _mlstm(p, gs, wts["mh_g"], batch=batch, seq=seq)
    mrg = _branches(p, ym, kv, wts["conv_w"], wts["w_branch"], batch=batch, seq=seq, tm=256)
    y = _out_proj(x2, mrg, wts["w_out"], wts["final_g"], tm=512)
    return y.reshape(batch, seq, d)


def _drop_gate_columns_kernel(w_ref, o_ref, *, g0):
    o_ref[:, :g0] = w_ref[:, :g0].astype(bf16)
    o_ref[:, g0:] = w_ref[:, g0 + N_GATE:].astype(bf16)


def _drop_gate_columns(w, g0, rb=64):
    d, n = w.shape
    assert d % rb == 0
    return pl.pallas_call(
        functools.partial(_drop_gate_columns_kernel, g0=g0),
        grid=(d // rb,),
        in_specs=[pl.BlockSpec((rb, n), lambda i: (i, 0))],
        out_specs=pl.BlockSpec((rb, n - N_GATE), lambda i: (i, 0)),
        out_shape=jax.ShapeDtypeStruct((d, n - N_GATE), bf16),
        compiler_params=pltpu.CompilerParams(dimension_semantics=("arbitrary",), vmem_limit_bytes=VMEM_LIMIT),
        name="drop_gate_columns",
    )(w)


def kernel(x_prompt, x_sample, mem_prompt, mem_sample, norm_g, w_in, b_if, conv_w, mem_norm_g, w_kv_mem,
           mh_norm_g, w_branch, w_out, final_norm_g):
    assert norm_g.shape[0] == 1, "single-layer trunk"
    w = w_in[0]
    g0 = 5 * D_MODEL
    w_main = _drop_gate_columns(w, g0)
    wg = w[:, g0:g0 + N_GATE].reshape(D_MODEL, 4, NH_M).transpose(2, 1, 0)
    wg = jnp.pad(wg, ((0, 0), (0, GATE_ROWS - 4), (0, 0))).reshape(NH_M * GATE_ROWS, D_MODEL).astype(bf16)
    bg = jnp.pad(b_if[0].reshape(4, NH_M).T, ((0, 0), (0, GATE_ROWS - 4))).reshape(NH_M * GATE_ROWS, 1)
    wts = dict(
        norm_g=norm_g[0][None, :], w_main=w_main, w_gate_t=wg, b_gate=bg.astype(f32),
        mem_g=mem_norm_g[0][None, :], w_kv=w_kv_mem[0].astype(bf16), mh_g=mh_norm_g[0][None, :],
        conv_w=conv_w[0], w_branch=w_branch[0].astype(bf16), w_out=w_out[0].astype(bf16),
        final_g=final_norm_g[None, :])
    return (_trunk(x_prompt, mem_prompt, wts), _trunk(x_sample, mem_sample, wts))
```

```python
import functools
import math

import jax
import jax.numpy as jnp
from jax import lax
from jax.experimental import pallas as pl
from jax.experimental.pallas import tpu as pltpu

D_MODEL = 2048
NH_M = 8
HD_M = D_MODEL // NH_M
CHUNK = 128
NH_A = 4
HD_A = D_MODEL // NH_A
N_GATE = 4 * NH_M
GATE_ROWS = 8
EPS = 1e-6
COL_QM, COL_KM, COL_VM, COL_OM, COL_ZM, COL_CB, COL_CC, COL_CX, COL_ZC, COL_QA, COL_ZA, COL_MG = range(12)
N_MAIN = 14 * D_MODEL
HALO = 16
VMEM_LIMIT = 56 * 1024 * 1024

f32 = jnp.float32
bf16 = jnp.bfloat16


def _sigmoid(x):
    return 0.5 * jnp.tanh(0.5 * x) + 0.5


def _nt_dot(a, b):
    return lax.dot_general(a, b, (((1,), (1,)), ((), ())), preferred_element_type=f32)


def _tn_dot(a, b):
    return lax.dot_general(a, b, (((0,), (0,)), ((), ())), preferred_element_type=f32)


def _norm_matmul_kernel(x_ref, g_ref, w_ref, *rest, with_gates, w_transposed):
    if with_gates:
        wgt_ref, bif_ref, p_ref, gt_ref, h_scr = rest
    else:
        p_ref, h_scr = rest

    @pl.when(pl.program_id(1) == 0)
    def _():
        xf = x_ref[...]
        ms = jnp.mean(xf * xf, axis=-1, keepdims=True)
        h = (xf * lax.rsqrt(ms + EPS) * g_ref[...]).astype(bf16)
        h_scr[...] = h
        if with_gates:
            gt_ref[...] = _nt_dot(wgt_ref[...], h) + bif_ref[...]

    dot = _nt_dot if w_transposed else functools.partial(jnp.dot, preferred_element_type=f32)
    p_ref[...] = dot(h_scr[...], w_ref[...]).astype(p_ref.dtype)


def _norm_matmul(x, g, w, wgt=None, bif=None, *, tm, tn, w_transposed=False):
    rows, d = x.shape
    n = w.shape[0] if w_transposed else w.shape[1]
    tm = min(tm, rows)
    tn = min(tn, n)
    assert rows % tm == 0 and n % tn == 0, (rows, n, tm, tn)
    with_gates = wgt is not None
    in_specs = [
        pl.BlockSpec((tm, d), lambda i, j: (i, 0)),
        pl.BlockSpec((1, d), lambda i, j: (0, 0)),
        pl.BlockSpec((tn, d), lambda i, j: (j, 0)) if w_transposed else pl.BlockSpec((d, tn), lambda i, j: (0, j)),
    ]
    out_shape = [jax.ShapeDtypeStruct((rows, n), bf16)]
    out_specs = [pl.BlockSpec((tm, tn), lambda i, j: (i, j))]
    args = [x, g, w]
    if with_gates:
        ng = wgt.shape[0]
        in_specs += [pl.BlockSpec((ng, d), lambda i, j: (0, 0)), pl.BlockSpec((ng, 1), lambda i, j: (0, 0))]
        out_shape.append(jax.ShapeDtypeStruct((ng, rows), f32))
        out_specs.append(pl.BlockSpec((ng, tm), lambda i, j: (0, i)))
        args += [wgt, bif]
    return pl.pallas_call(
        functools.partial(_norm_matmul_kernel, with_gates=with_gates, w_transposed=w_transposed),
        grid=(rows // tm, n // tn),
        in_specs=in_specs,
        out_specs=out_specs,
        out_shape=out_shape,
        scratch_shapes=[pltpu.VMEM((tm, d), bf16)],
        compiler_params=pltpu.CompilerParams(
            dimension_semantics=("arbitrary", "arbitrary"), vmem_limit_bytes=VMEM_LIMIT),
        name="norm_matmul_gates" if with_gates else "norm_matmul",
    )(*args)


def _lane_scan(x, op, reverse):
    lane = lax.broadcasted_iota(jnp.int32, x.shape, 1)
    n = x.shape[1]
    sh = 1
    while sh < n:
        if reverse:
            x = jnp.where(lane < n - sh, op(x, pltpu.roll(x, n - sh, axis=1)), x)
        else:
            x = jnp.where(lane >= sh, op(x, pltpu.roll(x, sh, axis=1)), x)
        sh *= 2
    return x


def _col_bcast(row):
    L = row.shape[1]
    return jnp.broadcast_to(row, (L, L)).T


GS_W, GS_G, GS_A, GS_ENM, GS_WK, GS_DECAY = range(6)


def _gate_scan_kernel(gt_ref, out_ref, *, nc):
    _, R, L = gt_ref.shape
    log_k_scale = -0.5 * math.log(HD_M)
    ck = lax.broadcasted_iota(jnp.int32, (R, L), 0) % nc
    for dr in range(2):
        rev = dr == 1
        i_pre, f_pre = gt_ref[dr], gt_ref[2 + dr]
        lf = jnp.minimum(f_pre, 0.0) - jnp.log1p(jnp.exp(-jnp.abs(f_pre)))
        b = _lane_scan(lf, jnp.add, rev)
        tot = b + _lane_scan(lf, jnp.add, not rev) - lf
        w = i_pre - b
        cw = _lane_scan(w, jnp.maximum, rev)
        mw = jnp.maximum(cw, _lane_scan(w, jnp.maximum, not rev))

        a, c = tot, mw + tot
        sh = 1
        while sh < nc:
            shift, valid = (R - sh, ck < nc - sh) if rev else (sh, ck >= sh)
            c = jnp.where(valid, jnp.maximum(pltpu.roll(c, shift, axis=0) + a, c), c)
            a = jnp.where(valid, pltpu.roll(a, shift, axis=0) + a, a)
            sh *= 2
        m_after = jnp.maximum(a, c)
        shift, valid = (R - 1, ck < nc - 1) if rev else (1, ck >= 1)
        m = jnp.where(valid, pltpu.roll(m_after, shift, axis=0), 0.0)

        g = jnp.maximum(cw, m)
        m_up = jnp.maximum(m, mw)
        w = w + log_k_scale
        out_ref[dr, GS_W] = w
        out_ref[dr, GS_G] = g
        out_ref[dr, GS_A] = jnp.exp(m - g)
        out_ref[dr, GS_ENM] = jnp.exp(-(g + b))
        out_ref[dr, GS_WK] = jnp.exp(w - m_up)
        out_ref[dr, GS_DECAY] = jnp.exp(m - m_up)


def _gate_scan(gt, *, nc):
    _, rows, L = gt.shape
    rb = min(rows, 256)
    assert rows % rb == 0 and rb % nc == 0 and rb % 8 == 0, (rows, nc)
    return pl.pallas_call(
        functools.partial(_gate_scan_kernel, nc=nc),
        grid=(rows // rb,),
        in_specs=[pl.BlockSpec((4, rb, L), lambda i: (0, i, 0))],
        out_specs=pl.BlockSpec((2, 6, rb, L), lambda i: (0, 0, i, 0)),
        out_shape=jax.ShapeDtypeStruct((2, 6, rows, L), f32),
        compiler_params=pltpu.CompilerParams(dimension_semantics=("arbitrary",)),
        name="gate_scan",
    )(gt)


def _mlstm_kernel(q_ref, k_ref, v_ref, o_ref, z_ref, gs_ref, mhg_ref, y_ref, h_scr, c_scr, n_scr):
    S, d = q_ref.shape
    L = CHUNK
    nc = S // L

    r = lax.broadcasted_iota(jnp.int32, (L, L), 0)
    c = lax.broadcasted_iota(jnp.int32, (L, L), 1)
    c_scr[...] = jnp.zeros_like(c_scr)
    n_scr[...] = jnp.zeros_like(n_scr)

    def chunk(ci, dr):
        causal = (c >= r) if dr == 1 else (c <= r)
        rows = pl.ds(pl.multiple_of(ci * L, L), L)
        one = pl.ds(ci, 1)
        qc, kc, vc = q_ref[rows, :], k_ref[rows, :], v_ref[rows, :]
        n8 = n_scr[dr]
        kn = jnp.concatenate([kc, jnp.broadcast_to(n8[0:1, :].astype(bf16), (L, d))], axis=0)
        qkn = _nt_dot(qc, kn)
        g = _col_bcast(gs_ref[dr, GS_G, one, :])
        a = _col_bcast(gs_ref[dr, GS_A, one, :])
        exp_neg_mt = _col_bcast(gs_ref[dr, GS_ENM, one, :])
        s = qkn[:, :L] * jnp.exp(jnp.where(causal, gs_ref[dr, GS_W, one, :] - g, -jnp.inf))
        cx = c_scr[dr]
        hx = (jnp.dot(s.astype(bf16), vc, preferred_element_type=f32)
              + jnp.concatenate([a, a], axis=1) * jnp.dot(qc, cx.astype(bf16), preferred_element_type=f32))
        den = jnp.maximum(jnp.abs(jnp.sum(s, axis=1, keepdims=True) + a * qkn[:, L:]), exp_neg_mt)
        rden = 1.0 / den
        h_scr[dr, rows, :] = hx * jnp.concatenate([rden, rden], axis=1)

        wk = gs_ref[dr, GS_WK, one, :]
        kw = kc.T * wk.astype(bf16)
        decay = gs_ref[dr, GS_DECAY, one, :]
        decay = jnp.concatenate([decay, decay], axis=1)
        c_scr[dr] = decay * cx + jnp.dot(kw, vc, preferred_element_type=f32)
        n_scr[dr] = decay * n8 + jnp.dot(jnp.broadcast_to(wk, (8, L)).astype(bf16), kc,
                                         preferred_element_type=f32)

    quarter_gain = 0.25 * mhg_ref[...]

    def finish(ci):
        rows = pl.ds(pl.multiple_of(ci * L, L), L)
        h = h_scr[0, rows, :] + h_scr[1, rows, :]
        mu = jnp.mean(h, axis=-1, keepdims=True)
        hc = h - mu
        var = jnp.mean(hc * hc, axis=-1, keepdims=True)
        hn = hc * lax.rsqrt(var + EPS) * quarter_gain
        o = o_ref[rows, :].astype(f32)
        z = z_ref[rows, :].astype(f32)
        gate = (1.0 + jnp.tanh(0.5 * o)) * ((1.0 + jnp.tanh(0.5 * z)) * z)
        y_ref[rows, :] = (hn * gate).astype(y_ref.dtype)

    def both(it, carry):
        chunk(it, 0)
        chunk(nc - 1 - it, 1)
        return carry

    def both_and_finish(it, carry):
        both(it, carry)
        finish(it)
        finish(nc - 1 - it)
        return carry

    def unroll(trips, most):
        return max(u for u in (1, 2, 4) if u <= most and trips % u == 0)

    half = nc // 2
    if half:
        lax.fori_loop(0, half, both, 0, unroll=unroll(half, 4))
    lax.fori_loop(half, nc, both_and_finish, 0, unroll=unroll(nc - half, 4))


def _mlstm(p, gs, mhg, *, batch, seq):
    nc = seq // CHUNK
    assert seq % CHUNK == 0 and nc % 8 == 0, seq

    def col(cb):
        return pl.BlockSpec((seq, HD_M), lambda b, h, cb=cb: (b, cb * NH_M + h))

    return pl.pallas_call(
        _mlstm_kernel,
        grid=(batch, NH_M),
        in_specs=[col(COL_QM), col(COL_KM), col(COL_VM), col(COL_OM), col(COL_ZM),
                  pl.BlockSpec((2, 6, nc, CHUNK), lambda b, h: (0, 0, b * NH_M + h, 0)),
                  pl.BlockSpec((1, HD_M), lambda b, h: (0, h))],
        out_specs=pl.BlockSpec((seq, HD_M), lambda b, h: (b, h)),
        out_shape=jax.ShapeDtypeStruct((batch * seq, D_MODEL), bf16),
        scratch_shapes=[pltpu.VMEM((2, seq, HD_M), f32), pltpu.VMEM((2, HD_M, HD_M), f32),
                        pltpu.VMEM((2, 8, HD_M), f32)],
        compiler_params=pltpu.CompilerParams(
            dimension_semantics=("arbitrary", "arbitrary"), vmem_limit_bytes=VMEM_LIMIT),
        name="mlstm",
    )(p, p, p, p, p, gs, mhg)


def _branches_kernel(cb_ref, cc_ref, cx_ref, zc_ref, ccp_ref, cxp_ref, ccn_ref, cxn_ref,
                     qa_ref, za_ref, mg0_ref, mg1_ref, mg2_ref, ym_ref, km_ref, vm_ref,
                     convw_ref, wb_ref, out_ref):
    tm = cb_ref.shape[0]
    i = pl.program_id(1)
    ni = pl.num_programs(1)

    u = cc_ref[...].astype(f32) * cx_ref[...].astype(f32)
    u_before = (ccp_ref[HALO - 1:HALO, :].astype(f32) * cxp_ref[HALO - 1:HALO, :].astype(f32)
                * (i > 0).astype(f32))
    u_after = ccn_ref[0:1, :].astype(f32) * cxn_ref[0:1, :].astype(f32) * (i < ni - 1).astype(f32)
    row = lax.broadcasted_iota(jnp.int32, (tm, 1), 0)
    u_prev = jnp.where(row == 0, u_before, pltpu.roll(u, 1, axis=0))
    u_next = jnp.where(row == tm - 1, u_after, pltpu.roll(u, tm - 1, axis=0))
    w = convw_ref[...]
    conv = w[0:1, :] * u_prev + w[1:2, :] * u + w[2:3, :] * u_next
    zc = zc_ref[...].astype(f32)
    y_c = cb_ref[...].astype(f32) * conv * (zc * _sigmoid(zc))

    heads = []
    for hd in range(NH_A):
        cols = slice(hd * HD_A, (hd + 1) * HD_A)
        s = _nt_dot(qa_ref[:, cols], km_ref[:, cols]) * (HD_A ** -0.5)
        s = s - jnp.max(s, axis=-1, keepdims=True)
        e = jnp.exp(s)
        p = e / jnp.sum(e, axis=-1, keepdims=True)
        heads.append(jnp.dot(p.astype(bf16), vm_ref[:, cols], preferred_element_type=f32))
    za = za_ref[...].astype(f32)
    y_a = jnp.concatenate(heads, axis=1) * (za * _sigmoid(za))

    def proj(y, k):
        return jnp.dot(y.astype(bf16), wb_ref[k], preferred_element_type=f32)

    merged = (_sigmoid(mg0_ref[...].astype(f32)) * proj(ym_ref[...], 0)
              + _sigmoid(mg1_ref[...].astype(f32)) * proj(y_c, 1)
              + _sigmoid(mg2_ref[...].astype(f32)) * proj(y_a, 2))
    out_ref[...] = merged.astype(out_ref.dtype)


def _branches(p, ym, kv, convw, wb, *, batch, seq, tm):
    tm = min(tm, seq)
    assert seq % tm == 0 and tm % HALO == 0, (seq, tm)
    nb = seq // tm
    hb = tm // HALO
    last_halo = batch * seq // HALO - 1

    def col(cb):
        return pl.BlockSpec((tm, D_MODEL), lambda b, i, cb=cb: (b * nb + i, cb))

    def halo_before(cb):
        return pl.BlockSpec((HALO, D_MODEL), lambda b, i, cb=cb: (jnp.maximum((b * nb + i) * hb - 1, 0), cb))

    def halo_after(cb):
        return pl.BlockSpec((HALO, D_MODEL),
                            lambda b, i, cb=cb: (jnp.minimum((b * nb + i + 1) * hb, last_halo), cb))

    n_mem = kv.shape[0] // batch
    const = lambda *shape: pl.BlockSpec(shape, lambda b, i: (0,) * len(shape), pipeline_mode=pl.Buffered(1))
    return pl.pallas_call(
        _branches_kernel,
        grid=(batch, nb),
        in_specs=[col(COL_CB), col(COL_CC), col(COL_CX), col(COL_ZC),
                  halo_before(COL_CC), halo_before(COL_CX), halo_after(COL_CC), halo_after(COL_CX),
                  col(COL_QA), col(COL_ZA), col(COL_MG), col(COL_MG + 1), col(COL_MG + 2),
                  pl.BlockSpec((tm, D_MODEL), lambda b, i: (b * nb + i, 0)),
                  pl.BlockSpec((n_mem, D_MODEL), lambda b, i: (b, 0)),
                  pl.BlockSpec((n_mem, D_MODEL), lambda b, i: (b, 1)),
                  const(3, D_MODEL), const(3, D_MODEL, D_MODEL)],
        out_specs=pl.BlockSpec((tm, D_MODEL), lambda b, i: (b * nb + i, 0)),
        out_shape=jax.ShapeDtypeStruct((batch * seq, D_MODEL), bf16),
        compiler_params=pltpu.CompilerParams(
            dimension_semantics=("arbitrary", "arbitrary"), vmem_limit_bytes=VMEM_LIMIT),
        name="branches",
    )(p, p, p, p, p, p, p, p, p, p, p, p, p, ym, kv, kv, convw, wb)


def _out_proj_kernel(x_ref, mrg_ref, w_ref, g_ref, y_ref):
    r = x_ref[...] + jnp.dot(mrg_ref[...], w_ref[...], preferred_element_type=f32)
    ms = jnp.mean(r * r, axis=-1, keepdims=True)
    y_ref[...] = r * lax.rsqrt(ms + EPS) * g_ref[...]


def _out_proj(x, mrg, w, g, *, tm):
    rows, d = x.shape
    tm = min(tm, rows)
    assert rows % tm == 0, (rows, tm)
    return pl.pallas_call(
        _out_proj_kernel,
        grid=(rows // tm,),
        in_specs=[pl.BlockSpec((tm, d), lambda i: (i, 0)),
                  pl.BlockSpec((tm, d), lambda i: (i, 0)),
                  pl.BlockSpec((d, d), lambda i: (0, 0)),
                  pl.BlockSpec((1, d), lambda i: (0, 0))],
        out_specs=pl.BlockSpec((tm, d), lambda i: (i, 0)),
        out_shape=jax.ShapeDtypeStruct((rows, d), f32),
        compiler_params=pltpu.CompilerParams(
            dimension_semantics=("arbitrary",), vmem_limit_bytes=VMEM_LIMIT),
        name="out_proj",
    )(x, mrg, w, g)


def _trunk(x, mem, wts):
    batch, seq, d = x.shape
    x2 = x.reshape(batch * seq, d)
    mem2 = mem.reshape(-1, d)
    p, gt = _norm_matmul(x2, wts["norm_g"], wts["w_main"], wts["w_gate_t"], wts["b_gate"], tm=1024, tn=2048,
                         w_transposed=True)
    kv, = _norm_matmul(mem2, wts["mem_g"], wts["w_kv"], tm=1024, tn=1024)
    nc = seq // CHUNK
    gt = gt.reshape(NH_M, GATE_ROWS, batch, nc, CHUNK)[:, :4].transpose(1, 2, 0, 3, 4)
    gs = _gate_scan(gt.reshape(4, batch * NH_M * nc, CHUNK), nc=nc)
    ym = _mlstm(p, gs, wts["mh_g"], batch=batch, seq=seq)
    mrg = _branches(p, ym, kv, wts["conv_w"], wts["w_branch"], batch=batch, seq=seq, tm=256)
    y = _out_proj(x2, mrg, wts["w_out"], wts["final_g"], tm=512)
    return y.reshape(batch, seq, d)


def _drop_gate_rows_kernel(a_ref, b_ref, o_ref, *, first_shifted):
    i = pl.program_id(0)

    @pl.when(i < first_shifted)
    def _():
        o_ref[...] = a_ref[...].astype(bf16)

    @pl.when(i >= first_shifted)
    def _():
        rb = a_ref.shape[0]
        o_ref[:rb - N_GATE, :] = a_ref[N_GATE:, :].astype(bf16)
        o_ref[rb - N_GATE:, :] = b_ref[...].astype(bf16)


def _drop_gate_rows(wt, g0, rb=1024):
    n, d = wt.shape
    n_out = n - N_GATE
    assert n_out % rb == 0 and g0 % rb == 0 and rb % N_GATE == 0 and N_GATE % 16 == 0
    return pl.pallas_call(
        functools.partial(_drop_gate_rows_kernel, first_shifted=g0 // rb),
        grid=(n_out // rb,),
        in_specs=[pl.BlockSpec((rb, d), lambda i: (i, 0)),
                  pl.BlockSpec((N_GATE, d), lambda i: ((i + 1) * (rb // N_GATE), 0))],
        out_specs=pl.BlockSpec((rb, d), lambda i: (i, 0)),
        out_shape=jax.ShapeDtypeStruct((n_out, d), bf16),
        compiler_params=pltpu.CompilerParams(dimension_semantics=("arbitrary",), vmem_limit_bytes=VMEM_LIMIT),
        name="drop_gate_rows",
    )(wt, wt)


def kernel(x_prompt, x_sample, mem_prompt, mem_sample, norm_g, w_in, b_if, conv_w, mem_norm_g, w_kv_mem,
           mh_norm_g, w_branch, w_out, final_norm_g):
    assert norm_g.shape[0] == 1, "single-layer trunk"
    wt = w_in[0].T
    g0 = 5 * D_MODEL
    w_main = _drop_gate_rows(wt, g0)
    wg = wt[g0:g0 + N_GATE].reshape(4, NH_M, D_MODEL).transpose(1, 0, 2)
    wg = jnp.pad(wg, ((0, 0), (0, GATE_ROWS - 4), (0, 0))).reshape(NH_M * GATE_ROWS, D_MODEL).astype(bf16)
    bg = jnp.pad(b_if[0].reshape(4, NH_M).T, ((0, 0), (0, GATE_ROWS - 4))).reshape(NH_M * GATE_ROWS, 1)
    wts = dict(
        norm_g=norm_g[0][None, :], w_main=w_main, w_gate_t=wg, b_gate=bg.astype(f32),
        mem_g=mem_norm_g[0][None, :], w_kv=w_kv_mem[0].astype(bf16), mh_g=mh_norm_g[0][None, :],
        conv_w=conv_w[0], w_branch=w_branch[0].astype(bf16), w_out=w_out[0].astype(bf16),
        final_g=final_norm_g[None, :])
    return (_trunk(x_prompt, mem_prompt, wts), _trunk(x_sample, mem_sample, wts))
```

```python
import functools
import math

import jax
import jax.numpy as jnp
from jax import lax
from jax.experimental import pallas as pl
from jax.experimental.pallas import tpu as pltpu

D_MODEL = 2048
NH_M = 8
HD_M = D_MODEL // NH_M
CHUNK = 128
NH_A = 4
HD_A = D_MODEL // NH_A
N_GATE = 4 * NH_M
GATE_ROWS = 8
EPS = 1e-6
COL_QM, COL_KM, COL_VM, COL_OM, COL_ZM, COL_CB, COL_CC, COL_CX, COL_ZC, COL_QA, COL_ZA, COL_MG = range(12)
N_MAIN = 14 * D_MODEL
HALO = 16
VMEM_LIMIT = 56 * 1024 * 1024

f32 = jnp.float32
bf16 = jnp.bfloat16


def _sigmoid(x):
    return 0.5 * jnp.tanh(0.5 * x) + 0.5


def _silu(x):
    half = 0.5 * x
    return half * jnp.tanh(half) + half


_ACTIVATIONS = {"id": lambda x: x, "sigmoid": _sigmoid, "silu": _silu}
IN_PROJ_ACT = ("id", "id", "id", "sigmoid", "silu",
               "id", "id", "id", "silu",
               "id", "silu", "sigmoid", "sigmoid", "sigmoid")


def _nt_dot(a, b):
    return lax.dot_general(a, b, (((1,), (1,)), ((), ())), preferred_element_type=f32)


def _tn_dot(a, b):
    return lax.dot_general(a, b, (((0,), (0,)), ((), ())), preferred_element_type=f32)


def _norm_matmul_kernel(x_ref, g_ref, w_ref, *rest, with_gates, w_transposed, col_act):
    if with_gates:
        wgt_ref, bif_ref, p_ref, gt_ref, h_scr = rest
    else:
        p_ref, h_scr = rest

    @pl.when(pl.program_id(1) == 0)
    def _():
        xf = x_ref[...]
        ms = jnp.mean(xf * xf, axis=-1, keepdims=True)
        h = (xf * lax.rsqrt(ms + EPS) * g_ref[...]).astype(bf16)
        h_scr[...] = h
        if with_gates:
            gt_ref[...] = _nt_dot(wgt_ref[...], h) + bif_ref[...]

    dot = _nt_dot if w_transposed else functools.partial(jnp.dot, preferred_element_type=f32)

    def emit(act):
        p_ref[...] = act(dot(h_scr[...], w_ref[...])).astype(p_ref.dtype)

    if col_act is None:
        emit(_ACTIVATIONS["id"])
    else:
        j = pl.program_id(1)
        for name in sorted(set(col_act)):
            hit = functools.reduce(jnp.logical_or, [j == jj for jj, a in enumerate(col_act) if a == name])
            pl.when(hit)(functools.partial(emit, _ACTIVATIONS[name]))


def _norm_matmul(x, g, w, wgt=None, bif=None, *, tm, tn, w_transposed=False, col_act=None):
    rows, d = x.shape
    n = w.shape[0] if w_transposed else w.shape[1]
    tm = min(tm, rows)
    tn = min(tn, n)
    assert rows % tm == 0 and n % tn == 0, (rows, n, tm, tn)
    assert col_act is None or len(col_act) == n // tn
    with_gates = wgt is not None
    in_specs = [
        pl.BlockSpec((tm, d), lambda i, j: (i, 0)),
        pl.BlockSpec((1, d), lambda i, j: (0, 0)),
        pl.BlockSpec((tn, d), lambda i, j: (j, 0)) if w_transposed else pl.BlockSpec((d, tn), lambda i, j: (0, j)),
    ]
    out_shape = [jax.ShapeDtypeStruct((rows, n), bf16)]
    out_specs = [pl.BlockSpec((tm, tn), lambda i, j: (i, j))]
    args = [x, g, w]
    if with_gates:
        ng = wgt.shape[0]
        in_specs += [pl.BlockSpec((ng, d), lambda i, j: (0, 0)), pl.BlockSpec((ng, 1), lambda i, j: (0, 0))]
        out_shape.append(jax.ShapeDtypeStruct((ng, rows), f32))
        out_specs.append(pl.BlockSpec((ng, tm), lambda i, j: (0, i)))
        args += [wgt, bif]
    return pl.pallas_call(
        functools.partial(_norm_matmul_kernel, with_gates=with_gates, w_transposed=w_transposed, col_act=col_act),
        grid=(rows // tm, n // tn),
        in_specs=in_specs,
        out_specs=out_specs,
        out_shape=out_shape,
        scratch_shapes=[pltpu.VMEM((tm, d), bf16)],
        compiler_params=pltpu.CompilerParams(
            dimension_semantics=("arbitrary", "arbitrary"), vmem_limit_bytes=VMEM_LIMIT),
        name="norm_matmul_gates" if with_gates else "norm_matmul",
    )(*args)


def _lane_scan(x, op, reverse):
    lane = lax.broadcasted_iota(jnp.int32, x.shape, 1)
    n = x.shape[1]
    sh = 1
    while sh < n:
        if reverse:
            x = jnp.where(lane < n - sh, op(x, pltpu.roll(x, n - sh, axis=1)), x)
        else:
            x = jnp.where(lane >= sh, op(x, pltpu.roll(x, sh, axis=1)), x)
        sh *= 2
    return x


def _col_bcast(row):
    L = row.shape[1]
    return jnp.broadcast_to(row, (L, L)).T


GS_W, GS_G, GS_A, GS_ENM, GS_WK, GS_DECAY = range(6)


def _gate_scan_kernel(gt_ref, out_ref, *, nc):
    _, R, L = gt_ref.shape
    log_k_scale = -0.5 * math.log(HD_M)
    ck = lax.broadcasted_iota(jnp.int32, (R, L), 0) % nc
    for dr in range(2):
        rev = dr == 1
        i_pre, f_pre = gt_ref[dr], gt_ref[2 + dr]
        lf = jnp.minimum(f_pre, 0.0) - jnp.log1p(jnp.exp(-jnp.abs(f_pre)))
        b = _lane_scan(lf, jnp.add, rev)
        tot = b + _lane_scan(lf, jnp.add, not rev) - lf
        w = i_pre - b
        cw = _lane_scan(w, jnp.maximum, rev)
        mw = jnp.maximum(cw, _lane_scan(w, jnp.maximum, not rev))

        a, c = tot, mw + tot
        sh = 1
        while sh < nc:
            shift, valid = (R - sh, ck < nc - sh) if rev else (sh, ck >= sh)
            c = jnp.where(valid, jnp.maximum(pltpu.roll(c, shift, axis=0) + a, c), c)
            a = jnp.where(valid, pltpu.roll(a, shift, axis=0) + a, a)
            sh *= 2
        m_after = jnp.maximum(a, c)
        shift, valid = (R - 1, ck < nc - 1) if rev else (1, ck >= 1)
        m = jnp.where(valid, pltpu.roll(m_after, shift, axis=0), 0.0)

        g = jnp.maximum(cw, m)
        m_up = jnp.maximum(m, mw)
        w = w + log_k_scale
        out_ref[dr, GS_W] = w
        out_ref[dr, GS_G] = g
        out_ref[dr, GS_A] = jnp.exp(m - g)
        out_ref[dr, GS_ENM] = jnp.exp(-(g + b))
        out_ref[dr, GS_WK] = jnp.exp(w - m_up)
        out_ref[dr, GS_DECAY] = jnp.exp(m - m_up)


def _gate_scan(gt, *, nc):
    _, rows, L = gt.shape
    rb = min(rows, 256)
    assert rows % rb == 0 and rb % nc == 0 and rb % 8 == 0, (rows, nc)
    return pl.pallas_call(
        functools.partial(_gate_scan_kernel, nc=nc),
        grid=(rows // rb,),
        in_specs=[pl.BlockSpec((4, rb, L), lambda i: (0, i, 0))],
        out_specs=pl.BlockSpec((2, 6, rb, L), lambda i: (0, 0, i, 0)),
        out_shape=jax.ShapeDtypeStruct((2, 6, rows, L), f32),
        compiler_params=pltpu.CompilerParams(dimension_semantics=("arbitrary",)),
        name="gate_scan",
    )(gt)


def _mlstm_kernel(q_ref, k_ref, v_ref, o_ref, z_ref, gs_ref, mhg_ref, y_ref, h_scr, c_scr, n_scr):
    S, d = q_ref.shape
    L = CHUNK
    nc = S // L

    r = lax.broadcasted_iota(jnp.int32, (L, L), 0)
    c = lax.broadcasted_iota(jnp.int32, (L, L), 1)
    c_scr[...] = jnp.zeros_like(c_scr)
    n_scr[...] = jnp.zeros_like(n_scr)

    def chunk(ci, dr):
        causal = (c >= r) if dr == 1 else (c <= r)
        rows = pl.ds(pl.multiple_of(ci * L, L), L)
        one = pl.ds(ci, 1)
        qc, kc, vc = q_ref[rows, :], k_ref[rows, :], v_ref[rows, :]
        n8 = n_scr[dr]
        kn = jnp.concatenate([kc, jnp.broadcast_to(n8[0:1, :].astype(bf16), (L, d))], axis=0)
        qkn = _nt_dot(qc, kn)
        g = _col_bcast(gs_ref[dr, GS_G, one, :])
        a = _col_bcast(gs_ref[dr, GS_A, one, :])
        exp_neg_mt = _col_bcast(gs_ref[dr, GS_ENM, one, :])
        s = qkn[:, :L] * jnp.exp(jnp.where(causal, gs_ref[dr, GS_W, one, :] - g, -jnp.inf))
        cx = c_scr[dr]
        hx = (jnp.dot(s.astype(bf16), vc, preferred_element_type=f32)
              + jnp.concatenate([a, a], axis=1) * jnp.dot(qc, cx.astype(bf16), preferred_element_type=f32))
        den = jnp.maximum(jnp.abs(jnp.sum(s, axis=1, keepdims=True) + a * qkn[:, L:]), exp_neg_mt)
        rden = 1.0 / den
        h_scr[dr, rows, :] = hx * jnp.concatenate([rden, rden], axis=1)

        wk = gs_ref[dr, GS_WK, one, :]
        kw = kc.T * wk.astype(bf16)
        decay = gs_ref[dr, GS_DECAY, one, :]
        decay = jnp.concatenate([decay, decay], axis=1)
        c_scr[dr] = decay * cx + jnp.dot(kw, vc, preferred_element_type=f32)
        n_scr[dr] = decay * n8 + jnp.dot(jnp.broadcast_to(wk, (8, L)).astype(bf16), kc,
                                         preferred_element_type=f32)

    def finish(ci):
        rows = pl.ds(pl.multiple_of(ci * L, L), L)
        h = h_scr[0, rows, :] + h_scr[1, rows, :]
        mu = jnp.mean(h, axis=-1, keepdims=True)
        hc = h - mu
        var = jnp.mean(hc * hc, axis=-1, keepdims=True)
        hn = hc * lax.rsqrt(var + EPS) * mhg_ref[...]
        y_ref[rows, :] = hn.astype(bf16) * (o_ref[rows, :] * z_ref[rows, :])

    def both(it, carry):
        chunk(it, 0)
        chunk(nc - 1 - it, 1)
        return carry

    def both_and_finish(it, carry):
        both(it, carry)
        finish(it)
        finish(nc - 1 - it)
        return carry

    def unroll(trips, most):
        return max(u for u in (1, 2, 4) if u <= most and trips % u == 0)

    half = nc // 2
    if half:
        lax.fori_loop(0, half, both, 0, unroll=unroll(half, 4))
    lax.fori_loop(half, nc, both_and_finish, 0, unroll=unroll(nc - half, 4))


def _mlstm(p, gs, mhg, *, batch, seq):
    nc = seq // CHUNK
    assert seq % CHUNK == 0 and nc % 8 == 0, seq

    def col(cb):
        return pl.BlockSpec((seq, HD_M), lambda b, h, cb=cb: (b, cb * NH_M + h))

    return pl.pallas_call(
        _mlstm_kernel,
        grid=(batch, NH_M),
        in_specs=[col(COL_QM), col(COL_KM), col(COL_VM), col(COL_OM), col(COL_ZM),
                  pl.BlockSpec((2, 6, nc, CHUNK), lambda b, h: (0, 0, b * NH_M + h, 0)),
                  pl.BlockSpec((1, HD_M), lambda b, h: (0, h))],
        out_specs=pl.BlockSpec((seq, HD_M), lambda b, h: (b, h)),
        out_shape=jax.ShapeDtypeStruct((batch * seq, D_MODEL), bf16),
        scratch_shapes=[pltpu.VMEM((2, seq, HD_M), f32), pltpu.VMEM((2, HD_M, HD_M), f32),
                        pltpu.VMEM((2, 8, HD_M), f32)],
        compiler_params=pltpu.CompilerParams(
            dimension_semantics=("arbitrary", "arbitrary"), vmem_limit_bytes=VMEM_LIMIT),
        name="mlstm",
    )(p, p, p, p, p, gs, mhg)


def _branches_kernel(cb_ref, cc_ref, cx_ref, zc_ref, ccp_ref, cxp_ref, ccn_ref, cxn_ref,
                     qa_ref, za_ref, mg0_ref, mg1_ref, mg2_ref, ym_ref, km_ref, vm_ref,
                     convw_ref, wb_ref, out_ref):
    tm = cb_ref.shape[0]
    i = pl.program_id(1)
    ni = pl.num_programs(1)

    def gated_proj(y, k, gate_ref):
        return gate_ref[...] * jnp.dot(y, wb_ref[k], preferred_element_type=f32).astype(bf16)

    merged = gated_proj(ym_ref[...], 0, mg0_ref)

    u = (cc_ref[...] * cx_ref[...]).astype(f32)
    u_before = (ccp_ref[HALO - 1:HALO, :] * cxp_ref[HALO - 1:HALO, :]).astype(f32) * (i > 0).astype(f32)
    u_after = (ccn_ref[0:1, :] * cxn_ref[0:1, :]).astype(f32) * (i < ni - 1).astype(f32)
    row = lax.broadcasted_iota(jnp.int32, (tm, 1), 0)
    u_prev = jnp.where(row == 0, u_before, pltpu.roll(u, 1, axis=0))
    u_next = jnp.where(row == tm - 1, u_after, pltpu.roll(u, tm - 1, axis=0))
    w = convw_ref[...]
    conv = w[0:1, :] * u_prev + w[1:2, :] * u + w[2:3, :] * u_next
    y_c = conv.astype(bf16) * (cb_ref[...] * zc_ref[...])
    merged = merged + gated_proj(y_c, 1, mg1_ref)

    heads = []
    for hd in range(NH_A):
        cols = slice(hd * HD_A, (hd + 1) * HD_A)
        s = _nt_dot(qa_ref[:, cols], km_ref[:, cols]) * (HD_A ** -0.5)
        s = s - jnp.max(s, axis=-1, keepdims=True)
        e = jnp.exp(s)
        p = e / jnp.sum(e, axis=-1, keepdims=True)
        heads.append(jnp.dot(p.astype(bf16), vm_ref[:, cols], preferred_element_type=f32))
    y_a = jnp.concatenate(heads, axis=1).astype(bf16) * za_ref[...]
    merged = merged + gated_proj(y_a, 2, mg2_ref)
    out_ref[...] = merged


def _branches(p, ym, kv, convw, wb, *, batch, seq, tm):
    tm = min(tm, seq)
    assert seq % tm == 0 and tm % HALO == 0, (seq, tm)
    nb = seq // tm
    hb = tm // HALO
    last_halo = batch * seq // HALO - 1

    def col(cb):
        return pl.BlockSpec((tm, D_MODEL), lambda b, i, cb=cb: (b * nb + i, cb))

    def halo_before(cb):
        return pl.BlockSpec((HALO, D_MODEL), lambda b, i, cb=cb: (jnp.maximum((b * nb + i) * hb - 1, 0), cb))

    def halo_after(cb):
        return pl.BlockSpec((HALO, D_MODEL),
                            lambda b, i, cb=cb: (jnp.minimum((b * nb + i + 1) * hb, last_halo), cb))

    n_mem = kv.shape[0] // batch
    const = lambda *shape: pl.BlockSpec(shape, lambda b, i: (0,) * len(shape), pipeline_mode=pl.Buffered(1))
    return pl.pallas_call(
        _branches_kernel,
        grid=(batch, nb),
        in_specs=[col(COL_CB), col(COL_CC), col(COL_CX), col(COL_ZC),
                  halo_before(COL_CC), halo_before(COL_CX), halo_after(COL_CC), halo_after(COL_CX),
                  col(COL_QA), col(COL_ZA), col(COL_MG), col(COL_MG + 1), col(COL_MG + 2),
                  pl.BlockSpec((tm, D_MODEL), lambda b, i: (b * nb + i, 0)),
                  pl.BlockSpec((n_mem, D_MODEL), lambda b, i: (b, 0)),
                  pl.BlockSpec((n_mem, D_MODEL), lambda b, i: (b, 1)),
                  const(3, D_MODEL), const(3, D_MODEL, D_MODEL)],
        out_specs=pl.BlockSpec((tm, D_MODEL), lambda b, i: (b * nb + i, 0)),
        out_shape=jax.ShapeDtypeStruct((batch * seq, D_MODEL), bf16),
        compiler_params=pltpu.CompilerParams(
            dimension_semantics=("arbitrary", "arbitrary"), vmem_limit_bytes=VMEM_LIMIT),
        name="branches",
    )(p, p, p, p, p, p, p, p, p, p, p, p, p, ym, kv, kv, convw, wb)


def _out_proj_kernel(x_ref, mrg_ref, w_ref, g_ref, y_ref):
    r = x_ref[...] + jnp.dot(mrg_ref[...], w_ref[...], preferred_element_type=f32)
    ms = jnp.mean(r * r, axis=-1, keepdims=True)
    y_ref[...] = r * lax.rsqrt(ms + EPS) * g_ref[...]


def _out_proj(x, mrg, w, g, *, tm):
    rows, d = x.shape
    tm = min(tm, rows)
    assert rows % tm == 0, (rows, tm)
    return pl.pallas_call(
        _out_proj_kernel,
        grid=(rows // tm,),
        in_specs=[pl.BlockSpec((tm, d), lambda i: (i, 0)),
                  pl.BlockSpec((tm, d), lambda i: (i, 0)),
                  pl.BlockSpec((d, d), lambda i: (0, 0)),
                  pl.BlockSpec((1, d), lambda i: (0, 0))],
        out_specs=pl.BlockSpec((tm, d), lambda i: (i, 0)),
        out_shape=jax.ShapeDtypeStruct((rows, d), f32),
        compiler_params=pltpu.CompilerParams(
            dimension_semantics=("arbitrary",), vmem_limit_bytes=VMEM_LIMIT),
        name="out_proj",
    )(x, mrg, w, g)


def _trunk(x, mem, wts):
    batch, seq, d = x.shape
    x2 = x.reshape(batch * seq, d)
    mem2 = mem.reshape(-1, d)
    p, gt = _norm_matmul(x2, wts["norm_g"], wts["w_main"], wts["w_gate_t"], wts["b_gate"], tm=1024, tn=D_MODEL,
                         w_transposed=True, col_act=IN_PROJ_ACT)
    kv, = _norm_matmul(mem2, wts["mem_g"], wts["w_kv"], tm=1024, tn=1024)
    nc = seq // CHUNK
    gt = gt.reshape(NH_M, GATE_ROWS, batch, nc, CHUNK)[:, :4].transpose(1, 2, 0, 3, 4)
    gs = _gate_scan(gt.reshape(4, batch * NH_M * nc, CHUNK), nc=nc)
    ym = _mlstm(p, gs, wts["mh_g"], batch=batch, seq=seq)
    mrg = _branches(p, ym, kv, wts["conv_w"], wts["w_branch"], batch=batch, seq=seq, tm=256)
    y = _out_proj(x2, mrg, wts["w_out"], wts["final_g"], tm=512)
    return y.reshape(batch, seq, d)


def _drop_gate_rows_kernel(a_ref, b_ref, o_ref, *, first_shifted):
    i = pl.program_id(0)

    @pl.when(i < first_shifted)
    def _():
        o_ref[...] = a_ref[...].astype(bf16)

    @pl.when(i >= first_shifted)
    def _():
        rb = a_ref.shape[0]
        o_ref[:rb - N_GATE, :] = a_ref[N_GATE:, :].astype(bf16)
        o_ref[rb - N_GATE:, :] = b_ref[...].astype(bf16)


def _drop_gate_rows(wt, g0, rb=1024):
    n, d = wt.shape
    n_out = n - N_GATE
    assert n_out % rb == 0 and g0 % rb == 0 and rb % N_GATE == 0 and N_GATE % 16 == 0
    return pl.pallas_call(
        functools.partial(_drop_gate_rows_kernel, first_shifted=g0 // rb),
        grid=(n_out // rb,),
        in_specs=[pl.BlockSpec((rb, d), lambda i: (i, 0)),
                  pl.BlockSpec((N_GATE, d), lambda i: ((i + 1) * (rb // N_GATE), 0))],
        out_specs=pl.BlockSpec((rb, d), lambda i: (i, 0)),
        out_shape=jax.ShapeDtypeStruct((n_out, d), bf16),
        compiler_params=pltpu.CompilerParams(dimension_semantics=("arbitrary",), vmem_limit_bytes=VMEM_LIMIT),
        name="drop_gate_rows",
    )(wt, wt)


def kernel(x_prompt, x_sample, mem_prompt, mem_sample, norm_g, w_in, b_if, conv_w, mem_norm_g, w_kv_mem,
           mh_norm_g, w_branch, w_out, final_norm_g):
    assert norm_g.shape[0] == 1, "single-layer trunk"
    wt = w_in[0].T
    g0 = 5 * D_MODEL
    w_main = _drop_gate_rows(wt, g0)
    wg = wt[g0:g0 + N_GATE].reshape(4, NH_M, D_MODEL).transpose(1, 0, 2)
    wg = jnp.pad(wg, ((0, 0), (0, GATE_ROWS - 4), (0, 0))).reshape(NH_M * GATE_ROWS, D_MODEL).astype(bf16)
    bg = jnp.pad(b_if[0].reshape(4, NH_M).T, ((0, 0), (0, GATE_ROWS - 4))).reshape(NH_M * GATE_ROWS, 1)
    wts = dict(
        norm_g=norm_g[0][None, :], w_main=w_main, w_gate_t=wg, b_gate=bg.astype(f32),
        mem_g=mem_norm_g[0][None, :], w_kv=w_kv_mem[0].astype(bf16), mh_g=mh_norm_g[0][None, :],
        conv_w=conv_w[0], w_branch=w_branch[0].astype(bf16), w_out=w_out[0].astype(bf16),
        final_g=final_norm_g[None, :])
    return (_trunk(x_prompt, mem_prompt, wts), _trunk(x_sample, mem_sample, wts))
```

```python
import functools
import math

import jax
import jax.numpy as jnp
from jax import lax
from jax.experimental import pallas as pl
from jax.experimental.pallas import tpu as pltpu

D_MODEL = 2048
NH_M = 8
HD_M = D_MODEL // NH_M
CHUNK = 128
NH_A = 4
HD_A = D_MODEL // NH_A
N_GATE = 4 * NH_M
GATE_ROWS = 8
EPS = 1e-6
COL_QM, COL_KM, COL_VM, COL_OM, COL_ZM, COL_CB, COL_CC, COL_CX, COL_ZC, COL_QA, COL_ZA, COL_MG = range(12)
N_MAIN = 14 * D_MODEL
HALO = 16
VMEM_LIMIT = 56 * 1024 * 1024

f32 = jnp.float32
bf16 = jnp.bfloat16


def _sigmoid(x):
    return 0.5 * jnp.tanh(0.5 * x) + 0.5


def _silu(x):
    half = 0.5 * x
    return half * jnp.tanh(half) + half


_ACTIVATIONS = {"id": lambda x: x, "sigmoid": _sigmoid, "silu": _silu}
IN_PROJ_ACT = ("id", "id", "id", "sigmoid", "silu",
               "id", "id", "id", "silu",
               "id", "silu", "sigmoid", "sigmoid", "sigmoid")


def _nt_dot(a, b):
    return lax.dot_general(a, b, (((1,), (1,)), ((), ())), preferred_element_type=f32)


def _tn_dot(a, b):
    return lax.dot_general(a, b, (((0,), (0,)), ((), ())), preferred_element_type=f32)


def _norm_matmul_kernel(x_ref, g_ref, w_ref, *rest, with_gates, w_transposed, col_act):
    if with_gates:
        wgt_ref, bif_ref, p_ref, gt_ref, h_scr = rest
    else:
        p_ref, h_scr = rest

    @pl.when(pl.program_id(1) == 0)
    def _():
        xf = x_ref[...]
        ms = jnp.mean(xf * xf, axis=-1, keepdims=True)
        h = (xf * lax.rsqrt(ms + EPS) * g_ref[...]).astype(bf16)
        h_scr[...] = h
        if with_gates:
            gt_ref[...] = _nt_dot(wgt_ref[...], h) + bif_ref[...]

    dot = _nt_dot if w_transposed else functools.partial(jnp.dot, preferred_element_type=f32)

    def emit(act):
        p_ref[...] = act(dot(h_scr[...], w_ref[...])).astype(p_ref.dtype)

    if col_act is None:
        emit(_ACTIVATIONS["id"])
    else:
        j = pl.program_id(1)
        for name in sorted(set(col_act)):
            hit = functools.reduce(jnp.logical_or, [j == jj for jj, a in enumerate(col_act) if a == name])
            pl.when(hit)(functools.partial(emit, _ACTIVATIONS[name]))


def _norm_matmul(x, g, w, wgt=None, bif=None, *, tm, tn, w_transposed=False, col_act=None):
    rows, d = x.shape
    n = w.shape[0] if w_transposed else w.shape[1]
    tm = min(tm, rows)
    tn = min(tn, n)
    assert rows % tm == 0 and n % tn == 0, (rows, n, tm, tn)
    assert col_act is None or len(col_act) == n // tn
    with_gates = wgt is not None
    in_specs = [
        pl.BlockSpec((tm, d), lambda i, j: (i, 0)),
        pl.BlockSpec((1, d), lambda i, j: (0, 0)),
        pl.BlockSpec((tn, d), lambda i, j: (j, 0)) if w_transposed else pl.BlockSpec((d, tn), lambda i, j: (0, j)),
    ]
    out_shape = [jax.ShapeDtypeStruct((rows, n), bf16)]
    out_specs = [pl.BlockSpec((tm, tn), lambda i, j: (i, j))]
    args = [x, g, w]
    if with_gates:
        ng = wgt.shape[0]
        in_specs += [pl.BlockSpec((ng, d), lambda i, j: (0, 0)), pl.BlockSpec((ng, 1), lambda i, j: (0, 0))]
        out_shape.append(jax.ShapeDtypeStruct((ng, rows), f32))
        out_specs.append(pl.BlockSpec((ng, tm), lambda i, j: (0, i)))
        args += [wgt, bif]
    return pl.pallas_call(
        functools.partial(_norm_matmul_kernel, with_gates=with_gates, w_transposed=w_transposed, col_act=col_act),
        grid=(rows // tm, n // tn),
        in_specs=in_specs,
        out_specs=out_specs,
        out_shape=out_shape,
        scratch_shapes=[pltpu.VMEM((tm, d), bf16)],
        compiler_params=pltpu.CompilerParams(
            dimension_semantics=("arbitrary", "arbitrary"), vmem_limit_bytes=VMEM_LIMIT),
        name="norm_matmul_gates" if with_gates else "norm_matmul",
    )(*args)


def _lane_scan(x, op, reverse):
    lane = lax.broadcasted_iota(jnp.int32, x.shape, 1)
    n = x.shape[1]
    sh = 1
    while sh < n:
        if reverse:
            x = jnp.where(lane < n - sh, op(x, pltpu.roll(x, n - sh, axis=1)), x)
        else:
            x = jnp.where(lane >= sh, op(x, pltpu.roll(x, sh, axis=1)), x)
        sh *= 2
    return x


def _col_bcast(row):
    L = row.shape[1]
    return jnp.broadcast_to(row, (L, L)).T


GS_W, GS_G, GS_A, GS_ENM, GS_WK, GS_DECAY = range(6)


def _gate_scan_kernel(gt_ref, out_ref, *, nc):
    _, R, L = gt_ref.shape
    log_k_scale = -0.5 * math.log(HD_M)
    ck = lax.broadcasted_iota(jnp.int32, (R, L), 0) % nc
    for dr in range(2):
        rev = dr == 1
        i_pre, f_pre = gt_ref[dr], gt_ref[2 + dr]
        lf = jnp.minimum(f_pre, 0.0) - jnp.log1p(jnp.exp(-jnp.abs(f_pre)))
        b = _lane_scan(lf, jnp.add, rev)
        tot = b + _lane_scan(lf, jnp.add, not rev) - lf
        w = i_pre - b
        cw = _lane_scan(w, jnp.maximum, rev)
        mw = jnp.maximum(cw, _lane_scan(w, jnp.maximum, not rev))

        a, c = tot, mw + tot
        sh = 1
        while sh < nc:
            shift, valid = (R - sh, ck < nc - sh) if rev else (sh, ck >= sh)
            c = jnp.where(valid, jnp.maximum(pltpu.roll(c, shift, axis=0) + a, c), c)
            a = jnp.where(valid, pltpu.roll(a, shift, axis=0) + a, a)
            sh *= 2
        m_after = jnp.maximum(a, c)
        shift, valid = (R - 1, ck < nc - 1) if rev else (1, ck >= 1)
        m = jnp.where(valid, pltpu.roll(m_after, shift, axis=0), 0.0)

        g = jnp.maximum(cw, m)
        m_up = jnp.maximum(m, mw)
        w = w + log_k_scale
        out_ref[dr, GS_W] = w
        out_ref[dr, GS_G] = g
        out_ref[dr, GS_A] = jnp.exp(m - g)
        out_ref[dr, GS_ENM] = jnp.exp(-(g + b))
        out_ref[dr, GS_WK] = jnp.exp(w - m_up)
        out_ref[dr, GS_DECAY] = jnp.exp(m - m_up)


def _gate_scan(gt, *, nc):
    _, rows, L = gt.shape
    rb = min(rows, 256)
    assert rows % rb == 0 and rb % nc == 0 and rb % 8 == 0, (rows, nc)
    return pl.pallas_call(
        functools.partial(_gate_scan_kernel, nc=nc),
        grid=(rows // rb,),
        in_specs=[pl.BlockSpec((4, rb, L), lambda i: (0, i, 0))],
        out_specs=pl.BlockSpec((2, 6, rb, L), lambda i: (0, 0, i, 0)),
        out_shape=jax.ShapeDtypeStruct((2, 6, rows, L), f32),
        compiler_params=pltpu.CompilerParams(dimension_semantics=("arbitrary",)),
        name="gate_scan",
    )(gt)


def _mlstm_kernel(q_ref, k_ref, v_ref, o_ref, z_ref, gs_ref, mhg_ref, y_ref, h_scr, c_scr, n_scr):
    S, d = q_ref.shape
    L = CHUNK
    nc = S // L

    r = lax.broadcasted_iota(jnp.int32, (L, L), 0)
    c = lax.broadcasted_iota(jnp.int32, (L, L), 1)
    c_scr[...] = jnp.zeros_like(c_scr)
    n_scr[...] = jnp.zeros_like(n_scr)

    def chunk(ci, dr):
        causal = (c >= r) if dr == 1 else (c <= r)
        rows = pl.ds(ci * L, L)
        one = pl.ds(ci, 1)
        qc, kc, vc = q_ref[rows, :], k_ref[rows, :], v_ref[rows, :]
        n8 = n_scr[dr]
        kn = jnp.concatenate([kc, jnp.broadcast_to(n8[0:1, :].astype(bf16), (L, d))], axis=0)
        qkn = _nt_dot(qc, kn)
        g = _col_bcast(gs_ref[dr, GS_G, one, :])
        a = _col_bcast(gs_ref[dr, GS_A, one, :])
        exp_neg_mt = _col_bcast(gs_ref[dr, GS_ENM, one, :])
        s = qkn[:, :L] * jnp.exp(jnp.where(causal, gs_ref[dr, GS_W, one, :] - g, -jnp.inf))
        cx = c_scr[dr]
        hx = (jnp.dot(s.astype(bf16), vc, preferred_element_type=f32)
              + jnp.concatenate([a, a], axis=1) * jnp.dot(qc, cx.astype(bf16), preferred_element_type=f32))
        den = jnp.maximum(jnp.abs(jnp.sum(s, axis=1, keepdims=True) + a * qkn[:, L:]), exp_neg_mt)
        rden = 1.0 / den
        h_scr[dr, rows, :] = hx * jnp.concatenate([rden, rden], axis=1)

        wk = gs_ref[dr, GS_WK, one, :]
        kw = kc.T * wk.astype(bf16)
        decay = gs_ref[dr, GS_DECAY, one, :]
        decay = jnp.concatenate([decay, decay], axis=1)
        c_scr[dr] = decay * cx + jnp.dot(kw, vc, preferred_element_type=f32)
        n_scr[dr] = decay * n8 + jnp.dot(jnp.broadcast_to(wk, (8, L)).astype(bf16), kc,
                                         preferred_element_type=f32)

    def finish(ci):
        rows = pl.ds(ci * L, L)
        h = h_scr[0, rows, :] + h_scr[1, rows, :]
        mu = jnp.mean(h, axis=-1, keepdims=True)
        hc = h - mu
        var = jnp.mean(hc * hc, axis=-1, keepdims=True)
        hn = hc * lax.rsqrt(var + EPS) * mhg_ref[...]
        y_ref[rows, :] = hn.astype(bf16) * (o_ref[rows, :] * z_ref[rows, :])

    ready = []
    for it in range(nc):
        chunk(it, 0)
        chunk(nc - 1 - it, 1)
        for ci in ready:
            finish(ci)
        ready = sorted({it, nc - 1 - it}) if it >= nc // 2 else []
    for ci in ready:
        finish(ci)


def _mlstm(p, gs, mhg, *, batch, seq):
    nc = seq // CHUNK
    assert seq % CHUNK == 0 and nc % 8 == 0, seq

    def col(cb):
        return pl.BlockSpec((seq, HD_M), lambda b, h, cb=cb: (b, cb * NH_M + h))

    return pl.pallas_call(
        _mlstm_kernel,
        grid=(batch, NH_M),
        in_specs=[col(COL_QM), col(COL_KM), col(COL_VM), col(COL_OM), col(COL_ZM),
                  pl.BlockSpec((2, 6, nc, CHUNK), lambda b, h: (0, 0, b * NH_M + h, 0)),
                  pl.BlockSpec((1, HD_M), lambda b, h: (0, h))],
        out_specs=pl.BlockSpec((seq, HD_M), lambda b, h: (b, h)),
        out_shape=jax.ShapeDtypeStruct((batch * seq, D_MODEL), bf16),
        scratch_shapes=[pltpu.VMEM((2, seq, HD_M), f32), pltpu.VMEM((2, HD_M, HD_M), f32),
                        pltpu.VMEM((2, 8, HD_M), f32)],
        compiler_params=pltpu.CompilerParams(
            dimension_semantics=("arbitrary", "arbitrary"), vmem_limit_bytes=VMEM_LIMIT),
        name="mlstm",
    )(p, p, p, p, p, gs, mhg)


def _branches_kernel(cb_ref, cc_ref, cx_ref, zc_ref, ccp_ref, cxp_ref, ccn_ref, cxn_ref,
                     qa_ref, za_ref, mg0_ref, mg1_ref, mg2_ref, ym_ref, km_ref, vm_ref,
                     convw_ref, wb_ref, out_ref):
    tm = cb_ref.shape[0]
    i = pl.program_id(1)
    ni = pl.num_programs(1)

    def gated_proj(y, k, gate_ref):
        return gate_ref[...] * jnp.dot(y, wb_ref[k], preferred_element_type=f32).astype(bf16)

    merged = gated_proj(ym_ref[...], 0, mg0_ref)

    u = (cc_ref[...] * cx_ref[...]).astype(f32)
    u_before = (ccp_ref[HALO - 1:HALO, :] * cxp_ref[HALO - 1:HALO, :]).astype(f32) * (i > 0).astype(f32)
    u_after = (ccn_ref[0:1, :] * cxn_ref[0:1, :]).astype(f32) * (i < ni - 1).astype(f32)
    row = lax.broadcasted_iota(jnp.int32, (tm, 1), 0)
    u_prev = jnp.where(row == 0, u_before, pltpu.roll(u, 1, axis=0))
    u_next = jnp.where(row == tm - 1, u_after, pltpu.roll(u, tm - 1, axis=0))
    w = convw_ref[...]
    conv = w[0:1, :] * u_prev + w[1:2, :] * u + w[2:3, :] * u_next
    y_c = conv.astype(bf16) * (cb_ref[...] * zc_ref[...])
    merged = merged + gated_proj(y_c, 1, mg1_ref)

    heads = []
    for hd in range(NH_A):
        cols = slice(hd * HD_A, (hd + 1) * HD_A)
        s = _nt_dot(qa_ref[:, cols], km_ref[:, cols]) * (HD_A ** -0.5)
        s = s - jnp.max(s, axis=-1, keepdims=True)
        e = jnp.exp(s)
        p = e / jnp.sum(e, axis=-1, keepdims=True)
        heads.append(jnp.dot(p.astype(bf16), vm_ref[:, cols], preferred_element_type=f32))
    y_a = jnp.concatenate(heads, axis=1).astype(bf16) * za_ref[...]
    merged = merged + gated_proj(y_a, 2, mg2_ref)
    out_ref[...] = merged


def _branches(p, ym, kv, convw, wb, *, batch, seq, tm):
    tm = min(tm, seq)
    assert seq % tm == 0 and tm % HALO == 0, (seq, tm)
    nb = seq // tm
    hb = tm // HALO
    last_halo = batch * seq // HALO - 1

    def col(cb):
        return pl.BlockSpec((tm, D_MODEL), lambda b, i, cb=cb: (b * nb + i, cb))

    def halo_before(cb):
        return pl.BlockSpec((HALO, D_MODEL), lambda b, i, cb=cb: (jnp.maximum((b * nb + i) * hb - 1, 0), cb))

    def halo_after(cb):
        return pl.BlockSpec((HALO, D_MODEL),
                            lambda b, i, cb=cb: (jnp.minimum((b * nb + i + 1) * hb, last_halo), cb))

    n_mem = kv.shape[0] // batch
    const = lambda *shape: pl.BlockSpec(shape, lambda b, i: (0,) * len(shape), pipeline_mode=pl.Buffered(1))
    return pl.pallas_call(
        _branches_kernel,
        grid=(batch, nb),
        in_specs=[col(COL_CB), col(COL_CC), col(COL_CX), col(COL_ZC),
                  halo_before(COL_CC), halo_before(COL_CX), halo_after(COL_CC), halo_after(COL_CX),
                  col(COL_QA), col(COL_ZA), col(COL_MG), col(COL_MG + 1), col(COL_MG + 2),
                  pl.BlockSpec((tm, D_MODEL), lambda b, i: (b * nb + i, 0)),
                  pl.BlockSpec((n_mem, D_MODEL), lambda b, i: (b, 0)),
                  pl.BlockSpec((n_mem, D_MODEL), lambda b, i: (b, 1)),
                  const(3, D_MODEL), const(3, D_MODEL, D_MODEL)],
        out_specs=pl.BlockSpec((tm, D_MODEL), lambda b, i: (b * nb + i, 0)),
        out_shape=jax.ShapeDtypeStruct((batch * seq, D_MODEL), bf16),
        compiler_params=pltpu.CompilerParams(
            dimension_semantics=("arbitrary", "arbitrary"), vmem_limit_bytes=VMEM_LIMIT),
        name="branches",
    )(p, p, p, p, p, p, p, p, p, p, p, p, p, ym, kv, kv, convw, wb)


def _out_proj_kernel(x_ref, mrg_ref, w_ref, g_ref, y_ref):
    r = x_ref[...] + jnp.dot(mrg_ref[...], w_ref[...], preferred_element_type=f32)
    ms = jnp.mean(r * r, axis=-1, keepdims=True)
    y_ref[...] = r * lax.rsqrt(ms + EPS) * g_ref[...]


def _out_proj(x, mrg, w, g, *, tm):
    rows, d = x.shape
    tm = min(tm, rows)
    assert rows % tm == 0, (rows, tm)
    return pl.pallas_call(
        _out_proj_kernel,
        grid=(rows // tm,),
        in_specs=[pl.BlockSpec((tm, d), lambda i: (i, 0)),
                  pl.BlockSpec((tm, d), lambda i: (i, 0)),
                  pl.BlockSpec((d, d), lambda i: (0, 0)),
                  pl.BlockSpec((1, d), lambda i: (0, 0))],
        out_specs=pl.BlockSpec((tm, d), lambda i: (i, 0)),
        out_shape=jax.ShapeDtypeStruct((rows, d), f32),
        compiler_params=pltpu.CompilerParams(
            dimension_semantics=("arbitrary",), vmem_limit_bytes=VMEM_LIMIT),
        name="out_proj",
    )(x, mrg, w, g)


def _trunk(x, mem, wts):
    batch, seq, d = x.shape
    x2 = x.reshape(batch * seq, d)
    mem2 = mem.reshape(-1, d)
    p, gt = _norm_matmul(x2, wts["norm_g"], wts["w_main"], wts["w_gate_t"], wts["b_gate"], tm=1024, tn=D_MODEL,
                         w_transposed=True, col_act=IN_PROJ_ACT)
    kv, = _norm_matmul(mem2, wts["mem_g"], wts["w_kv"], tm=1024, tn=1024)
    nc = seq // CHUNK
    gt = gt.reshape(NH_M, GATE_ROWS, batch, nc, CHUNK)[:, :4].transpose(1, 2, 0, 3, 4)
    gs = _gate_scan(gt.reshape(4, batch * NH_M * nc, CHUNK), nc=nc)
    ym = _mlstm(p, gs, wts["mh_g"], batch=batch, seq=seq)
    mrg = _branches(p, ym, kv, wts["conv_w"], wts["w_branch"], batch=batch, seq=seq, tm=256)
    y = _out_proj(x2, mrg, wts["w_out"], wts["final_g"], tm=512)
    return y.reshape(batch, seq, d)


def _drop_gate_rows_kernel(a_ref, b_ref, o_ref, *, first_shifted):
    i = pl.program_id(0)

    @pl.when(i < first_shifted)
    def _():
        o_ref[...] = a_ref[...].astype(bf16)

    @pl.when(i >= first_shifted)
    def _():
        rb = a_ref.shape[0]
        o_ref[:rb - N_GATE, :] = a_ref[N_GATE:, :].astype(bf16)
        o_ref[rb - N_GATE:, :] = b_ref[...].astype(bf16)


def _drop_gate_rows(wt, g0, rb=1024):
    n, d = wt.shape
    n_out = n - N_GATE
    assert n_out % rb == 0 and g0 % rb == 0 and rb % N_GATE == 0 and N_GATE % 16 == 0
    return pl.pallas_call(
        functools.partial(_drop_gate_rows_kernel, first_shifted=g0 // rb),
        grid=(n_out // rb,),
        in_specs=[pl.BlockSpec((rb, d), lambda i: (i, 0)),
                  pl.BlockSpec((N_GATE, d), lambda i: ((i + 1) * (rb // N_GATE), 0))],
        out_specs=pl.BlockSpec((rb, d), lambda i: (i, 0)),
        out_shape=jax.ShapeDtypeStruct((n_out, d), bf16),
        compiler_params=pltpu.CompilerParams(dimension_semantics=("arbitrary",), vmem_limit_bytes=VMEM_LIMIT),
        name="drop_gate_rows",
    )(wt, wt)


def kernel(x_prompt, x_sample, mem_prompt, mem_sample, norm_g, w_in, b_if, conv_w, mem_norm_g, w_kv_mem,
           mh_norm_g, w_branch, w_out, final_norm_g):
    assert norm_g.shape[0] == 1, "single-layer trunk"
    wt = w_in[0].T
    g0 = 5 * D_MODEL
    w_main = _drop_gate_rows(wt, g0)
    wg = wt[g0:g0 + N_GATE].reshape(4, NH_M, D_MODEL).transpose(1, 0, 2)
    wg = jnp.pad(wg, ((0, 0), (0, GATE_ROWS - 4), (0, 0))).reshape(NH_M * GATE_ROWS, D_MODEL).astype(bf16)
    bg = jnp.pad(b_if[0].reshape(4, NH_M).T, ((0, 0), (0, GATE_ROWS - 4))).reshape(NH_M * GATE_ROWS, 1)
    wts = dict(
        norm_g=norm_g[0][None, :], w_main=w_main, w_gate_t=wg, b_gate=bg.astype(f32),
        mem_g=mem_norm_g[0][None, :], w_kv=w_kv_mem[0].astype(bf16), mh_g=mh_norm_g[0][None, :],
        conv_w=conv_w[0], w_branch=w_branch[0].astype(bf16), w_out=w_out[0].astype(bf16),
        final_g=final_norm_g[None, :])
    return (_trunk(x_prompt, mem_prompt, wts), _trunk(x_sample, mem_sample, wts))
```

```python
import functools
import math

import jax
import jax.numpy as jnp
from jax import lax
from jax.experimental import pallas as pl
from jax.experimental.pallas import tpu as pltpu

D_MODEL = 2048
NH_M = 8
HD_M = D_MODEL // NH_M
CHUNK = 128
NH_A = 4
HD_A = D_MODEL // NH_A
N_GATE = 4 * NH_M
GATE_ROWS = 8
EPS = 1e-6
COL_QM, COL_KM, COL_VM, COL_OM, COL_ZM, COL_CB, COL_CC, COL_CX, COL_ZC, COL_QA, COL_ZA, COL_MG = range(12)
N_MAIN = 14 * D_MODEL
HALO = 16
VMEM_LIMIT = 56 * 1024 * 1024

f32 = jnp.float32
bf16 = jnp.bfloat16


def _sigmoid(x):
    return 0.5 * jnp.tanh(0.5 * x) + 0.5


def _silu(x):
    half = 0.5 * x
    return half * jnp.tanh(half) + half


_ACTIVATIONS = {"id": lambda x: x, "sigmoid": _sigmoid, "silu": _silu}
IN_PROJ_ACT = ("id", "id", "id", "sigmoid", "silu",
               "id", "id", "id", "silu",
               "id", "silu", "sigmoid", "sigmoid", "sigmoid")


def _nt_dot(a, b):
    return lax.dot_general(a, b, (((1,), (1,)), ((), ())), preferred_element_type=f32)


def _tn_dot(a, b):
    return lax.dot_general(a, b, (((0,), (0,)), ((), ())), preferred_element_type=f32)


def _norm_matmul_kernel(x_ref, g_ref, w_ref, *rest, with_gates, w_transposed, col_act):
    if with_gates:
        wgt_ref, bif_ref, p_ref, gt_ref, h_scr = rest
    else:
        p_ref, h_scr = rest

    @pl.when(pl.program_id(1) == 0)
    def _():
        xf = x_ref[...]
        ms = jnp.mean(xf * xf, axis=-1, keepdims=True)
        h = (xf * lax.rsqrt(ms + EPS) * g_ref[...]).astype(bf16)
        h_scr[...] = h
        if with_gates:
            gt_ref[...] = _nt_dot(wgt_ref[...], h) + bif_ref[...]

    dot = _nt_dot if w_transposed else functools.partial(jnp.dot, preferred_element_type=f32)

    def emit(act):
        p_ref[...] = act(dot(h_scr[...], w_ref[...])).astype(p_ref.dtype)

    if col_act is None:
        emit(_ACTIVATIONS["id"])
    else:
        j = pl.program_id(1)
        for name in sorted(set(col_act)):
            hit = functools.reduce(jnp.logical_or, [j == jj for jj, a in enumerate(col_act) if a == name])
            pl.when(hit)(functools.partial(emit, _ACTIVATIONS[name]))


def _norm_matmul(x, g, w, wgt=None, bif=None, *, tm, tn, w_transposed=False, col_act=None):
    rows, d = x.shape
    n = w.shape[0] if w_transposed else w.shape[1]
    tm = min(tm, rows)
    tn = min(tn, n)
    assert rows % tm == 0 and n % tn == 0, (rows, n, tm, tn)
    assert col_act is None or len(col_act) == n // tn
    with_gates = wgt is not None
    in_specs = [
        pl.BlockSpec((tm, d), lambda i, j: (i, 0)),
        pl.BlockSpec((1, d), lambda i, j: (0, 0)),
        pl.BlockSpec((tn, d), lambda i, j: (j, 0)) if w_transposed else pl.BlockSpec((d, tn), lambda i, j: (0, j)),
    ]
    out_shape = [jax.ShapeDtypeStruct((rows, n), bf16)]
    out_specs = [pl.BlockSpec((tm, tn), lambda i, j: (i, j))]
    args = [x, g, w]
    if with_gates:
        ng = wgt.shape[0]
        in_specs += [pl.BlockSpec((ng, d), lambda i, j: (0, 0)), pl.BlockSpec((ng, 1), lambda i, j: (0, 0))]
        out_shape.append(jax.ShapeDtypeStruct((ng, rows), f32))
        out_specs.append(pl.BlockSpec((ng, tm), lambda i, j: (0, i)))
        args += [wgt, bif]
    return pl.pallas_call(
        functools.partial(_norm_matmul_kernel, with_gates=with_gates, w_transposed=w_transposed, col_act=col_act),
        grid=(rows // tm, n // tn),
        in_specs=in_specs,
        out_specs=out_specs,
        out_shape=out_shape,
        scratch_shapes=[pltpu.VMEM((tm, d), bf16)],
        compiler_params=pltpu.CompilerParams(
            dimension_semantics=("arbitrary", "arbitrary"), vmem_limit_bytes=VMEM_LIMIT),
        name="norm_matmul_gates" if with_gates else "norm_matmul",
    )(*args)


def _lane_scan(x, op, reverse):
    lane = lax.broadcasted_iota(jnp.int32, x.shape, 1)
    n = x.shape[1]
    sh = 1
    while sh < n:
        if reverse:
            x = jnp.where(lane < n - sh, op(x, pltpu.roll(x, n - sh, axis=1)), x)
        else:
            x = jnp.where(lane >= sh, op(x, pltpu.roll(x, sh, axis=1)), x)
        sh *= 2
    return x


def _col_bcast(row):
    L = row.shape[1]
    return jnp.broadcast_to(row, (L, L)).T


GS_W, GS_G, GS_A, GS_ENM, GS_WK, GS_DECAY = range(6)


def _gate_scan_kernel(gt_ref, out_ref, *, nc):
    _, R, L = gt_ref.shape
    log_k_scale = -0.5 * math.log(HD_M)
    ck = lax.broadcasted_iota(jnp.int32, (R, L), 0) % nc
    for dr in range(2):
        rev = dr == 1
        i_pre, f_pre = gt_ref[dr], gt_ref[2 + dr]
        lf = jnp.minimum(f_pre, 0.0) - jnp.log1p(jnp.exp(-jnp.abs(f_pre)))
        b = _lane_scan(lf, jnp.add, rev)
        tot = b + _lane_scan(lf, jnp.add, not rev) - lf
        w = i_pre - b
        cw = _lane_scan(w, jnp.maximum, rev)
        mw = jnp.maximum(cw, _lane_scan(w, jnp.maximum, not rev))

        a, c = tot, mw + tot
        sh = 1
        while sh < nc:
            shift, valid = (R - sh, ck < nc - sh) if rev else (sh, ck >= sh)
            c = jnp.where(valid, jnp.maximum(pltpu.roll(c, shift, axis=0) + a, c), c)
            a = jnp.where(valid, pltpu.roll(a, shift, axis=0) + a, a)
            sh *= 2
        m_after = jnp.maximum(a, c)
        shift, valid = (R - 1, ck < nc - 1) if rev else (1, ck >= 1)
        m = jnp.where(valid, pltpu.roll(m_after, shift, axis=0), 0.0)

        g = jnp.maximum(cw, m)
        m_up = jnp.maximum(m, mw)
        w = w + log_k_scale
        out_ref[dr, GS_W] = w
        out_ref[dr, GS_G] = g
        out_ref[dr, GS_A] = jnp.exp(m - g)
        out_ref[dr, GS_ENM] = jnp.exp(-(g + b))
        out_ref[dr, GS_WK] = jnp.exp(w - m_up)
        out_ref[dr, GS_DECAY] = jnp.exp(m - m_up)


def _gate_scan(gt, *, nc):
    _, rows, L = gt.shape
    rb = min(rows, 256)
    assert rows % rb == 0 and rb % nc == 0 and rb % 8 == 0, (rows, nc)
    return pl.pallas_call(
        functools.partial(_gate_scan_kernel, nc=nc),
        grid=(rows // rb,),
        in_specs=[pl.BlockSpec((4, rb, L), lambda i: (0, i, 0))],
        out_specs=pl.BlockSpec((2, 6, rb, L), lambda i: (0, 0, i, 0)),
        out_shape=jax.ShapeDtypeStruct((2, 6, rows, L), f32),
        compiler_params=pltpu.CompilerParams(dimension_semantics=("arbitrary",)),
        name="gate_scan",
    )(gt)


def _mlstm_kernel(q_ref, k_ref, v_ref, o_ref, z_ref, gs_ref, mhg_ref, y_ref, h_scr):
    S, d = q_ref.shape
    L = CHUNK
    nc = S // L

    r = lax.broadcasted_iota(jnp.int32, (L, L), 0)
    c = lax.broadcasted_iota(jnp.int32, (L, L), 1)

    def prepare(ci, dr):
        causal = (c >= r) if dr == 1 else (c <= r)
        rows = pl.ds(ci * L, L)
        one = pl.ds(ci, 1)
        qc, kc, vc = q_ref[rows, :], k_ref[rows, :], v_ref[rows, :]
        g = _col_bcast(gs_ref[dr, GS_G, one, :])
        a = _col_bcast(gs_ref[dr, GS_A, one, :])
        exp_neg_mt = _col_bcast(gs_ref[dr, GS_ENM, one, :])
        e = jnp.exp(jnp.where(causal, gs_ref[dr, GS_W, one, :] - g, -jnp.inf))
        wk = gs_ref[dr, GS_WK, one, :]
        kw = kc.T * wk.astype(bf16)
        decay = gs_ref[dr, GS_DECAY, one, :]
        decay = jnp.concatenate([decay, decay], axis=1)
        return rows, qc, kc, vc, a, exp_neg_mt, e, wk, kw, decay

    def advance(dr, prepared, cx, n8):
        rows, qc, kc, vc, a, exp_neg_mt, e, wk, kw, decay = prepared
        kn = jnp.concatenate([kc, jnp.broadcast_to(n8[0:1, :].astype(bf16), (L, d))], axis=0)
        qkn = _nt_dot(qc, kn)
        s = qkn[:, :L] * e
        hx = (jnp.dot(s.astype(bf16), vc, preferred_element_type=f32)
              + jnp.concatenate([a, a], axis=1) * jnp.dot(qc, cx.astype(bf16), preferred_element_type=f32))
        den = jnp.maximum(jnp.abs(jnp.sum(s, axis=1, keepdims=True) + a * qkn[:, L:]), exp_neg_mt)
        rden = 1.0 / den
        h_scr[dr, rows, :] = hx * jnp.concatenate([rden, rden], axis=1)
        return (decay * cx + jnp.dot(kw, vc, preferred_element_type=f32),
                decay * n8 + jnp.dot(jnp.broadcast_to(wk, (8, L)).astype(bf16), kc, preferred_element_type=f32))

    def finish(ci):
        rows = pl.ds(ci * L, L)
        h = h_scr[0, rows, :] + h_scr[1, rows, :]
        mu = jnp.mean(h, axis=-1, keepdims=True)
        hc = h - mu
        var = jnp.mean(hc * hc, axis=-1, keepdims=True)
        hn = hc * lax.rsqrt(var + EPS) * mhg_ref[...]
        y_ref[rows, :] = hn.astype(bf16) * (o_ref[rows, :] * z_ref[rows, :])

    ready = []
    state = [(jnp.zeros((d, d), f32), jnp.zeros((8, d), f32))] * 2
    prepared = [prepare(0, 0), prepare(nc - 1, 1)]
    for it in range(nc):
        following = [prepare(it + 1, 0), prepare(nc - 2 - it, 1)] if it + 1 < nc else None
        state = [advance(dr, prepared[dr], *state[dr]) for dr in range(2)]
        prepared = following
        for ci in ready:
            finish(ci)
        ready = sorted({it, nc - 1 - it}) if it >= nc // 2 else []
    for ci in ready:
        finish(ci)


def _mlstm(p, gs, mhg, *, batch, seq):
    nc = seq // CHUNK
    assert seq % CHUNK == 0 and nc % 8 == 0, seq

    def col(cb):
        return pl.BlockSpec((seq, HD_M), lambda b, h, cb=cb: (b, cb * NH_M + h))

    return pl.pallas_call(
        _mlstm_kernel,
        grid=(batch, NH_M),
        in_specs=[col(COL_QM), col(COL_KM), col(COL_VM), col(COL_OM), col(COL_ZM),
                  pl.BlockSpec((2, 6, nc, CHUNK), lambda b, h: (0, 0, b * NH_M + h, 0)),
                  pl.BlockSpec((1, HD_M), lambda b, h: (0, h))],
        out_specs=pl.BlockSpec((seq, HD_M), lambda b, h: (b, h)),
        out_shape=jax.ShapeDtypeStruct((batch * seq, D_MODEL), bf16),
        scratch_shapes=[pltpu.VMEM((2, seq, HD_M), f32)],
        compiler_params=pltpu.CompilerParams(
            dimension_semantics=("arbitrary", "arbitrary"), vmem_limit_bytes=VMEM_LIMIT),
        name="mlstm",
    )(p, p, p, p, p, gs, mhg)


def _branches_kernel(cb_ref, cc_ref, cx_ref, zc_ref, ccp_ref, cxp_ref, ccn_ref, cxn_ref,
                     qa_ref, za_ref, mg0_ref, mg1_ref, mg2_ref, ym_ref, km_ref, vm_ref,
                     convw_ref, wb_ref, out_ref):
    tm = cb_ref.shape[0]
    i = pl.program_id(1)
    ni = pl.num_programs(1)

    slabs = [slice(c0, c0 + HD_A) for c0 in range(0, D_MODEL, HD_A)]

    def gated_proj(y, k, gate_ref, cols):
        return gate_ref[:, cols] * jnp.dot(y, wb_ref[k, :, cols], preferred_element_type=f32).astype(bf16)

    row = lax.broadcasted_iota(jnp.int32, (tm, 1), 0)
    first = (i > 0).astype(f32)
    last = (i < ni - 1).astype(f32)

    def conv_branch(cols):
        u = (cc_ref[:, cols] * cx_ref[:, cols]).astype(f32)
        u_before = (ccp_ref[HALO - 1:HALO, cols] * cxp_ref[HALO - 1:HALO, cols]).astype(f32) * first
        u_after = (ccn_ref[0:1, cols] * cxn_ref[0:1, cols]).astype(f32) * last
        u_prev = jnp.where(row == 0, u_before, pltpu.roll(u, 1, axis=0))
        u_next = jnp.where(row == tm - 1, u_after, pltpu.roll(u, tm - 1, axis=0))
        w = convw_ref[:, cols]
        conv = w[0:1, :] * u_prev + w[1:2, :] * u + w[2:3, :] * u_next
        return conv.astype(bf16) * (cb_ref[:, cols] * zc_ref[:, cols])

    def attention_branch(cols):
        s = _nt_dot(qa_ref[:, cols], km_ref[:, cols]) * (HD_A ** -0.5)
        s = s - jnp.max(s, axis=-1, keepdims=True)
        e = jnp.exp(s)
        p = e / jnp.sum(e, axis=-1, keepdims=True)
        o = jnp.dot(p.astype(bf16), vm_ref[:, cols], preferred_element_type=f32)
        return o.astype(bf16) * za_ref[:, cols]

    ym = ym_ref[...]
    merged, y_c, y_a = [], [], []
    for cols in slabs:
        merged.append(gated_proj(ym, 0, mg0_ref, cols))
        y_c.append(conv_branch(cols))
    y_c = jnp.concatenate(y_c, axis=1)
    for n, cols in enumerate(slabs):
        merged[n] = merged[n] + gated_proj(y_c, 1, mg1_ref, cols)
        y_a.append(attention_branch(cols))
    y_a = jnp.concatenate(y_a, axis=1)
    for n, cols in enumerate(slabs):
        out_ref[:, cols] = merged[n] + gated_proj(y_a, 2, mg2_ref, cols)


def _branches(p, ym, kv, convw, wb, *, batch, seq, tm):
    tm = min(tm, seq)
    assert seq % tm == 0 and tm % HALO == 0, (seq, tm)
    nb = seq // tm
    hb = tm // HALO
    last_halo = batch * seq // HALO - 1

    def col(cb):
        return pl.BlockSpec((tm, D_MODEL), lambda b, i, cb=cb: (b * nb + i, cb))

    def halo_before(cb):
        return pl.BlockSpec((HALO, D_MODEL), lambda b, i, cb=cb: (jnp.maximum((b * nb + i) * hb - 1, 0), cb))

    def halo_after(cb):
        return pl.BlockSpec((HALO, D_MODEL),
                            lambda b, i, cb=cb: (jnp.minimum((b * nb + i + 1) * hb, last_halo), cb))

    n_mem = kv.shape[0] // batch
    const = lambda *shape: pl.BlockSpec(shape, lambda b, i: (0,) * len(shape), pipeline_mode=pl.Buffered(1))
    return pl.pallas_call(
        _branches_kernel,
        grid=(batch, nb),
        in_specs=[col(COL_CB), col(COL_CC), col(COL_CX), col(COL_ZC),
                  halo_before(COL_CC), halo_before(COL_CX), halo_after(COL_CC), halo_after(COL_CX),
                  col(COL_QA), col(COL_ZA), col(COL_MG), col(COL_MG + 1), col(COL_MG + 2),
                  pl.BlockSpec((tm, D_MODEL), lambda b, i: (b * nb + i, 0)),
                  pl.BlockSpec((n_mem, D_MODEL), lambda b, i: (b, 0)),
                  pl.BlockSpec((n_mem, D_MODEL), lambda b, i: (b, 1)),
                  const(3, D_MODEL), const(3, D_MODEL, D_MODEL)],
        out_specs=pl.BlockSpec((tm, D_MODEL), lambda b, i: (b * nb + i, 0)),
        out_shape=jax.ShapeDtypeStruct((batch * seq, D_MODEL), bf16),
        compiler_params=pltpu.CompilerParams(
            dimension_semantics=("arbitrary", "arbitrary"), vmem_limit_bytes=VMEM_LIMIT),
        name="branches",
    )(p, p, p, p, p, p, p, p, p, p, p, p, p, ym, kv, kv, convw, wb)


def _out_proj_kernel(x_ref, mrg_ref, w_ref, g_ref, y_ref):
    r = x_ref[...] + jnp.dot(mrg_ref[...], w_ref[...], preferred_element_type=f32)
    ms = jnp.mean(r * r, axis=-1, keepdims=True)
    y_ref[...] = r * lax.rsqrt(ms + EPS) * g_ref[...]


def _out_proj(x, mrg, w, g, *, tm):
    rows, d = x.shape
    tm = min(tm, rows)
    assert rows % tm == 0, (rows, tm)
    return pl.pallas_call(
        _out_proj_kernel,
        grid=(rows // tm,),
        in_specs=[pl.BlockSpec((tm, d), lambda i: (i, 0)),
                  pl.BlockSpec((tm, d), lambda i: (i, 0)),
                  pl.BlockSpec((d, d), lambda i: (0, 0)),
                  pl.BlockSpec((1, d), lambda i: (0, 0))],
        out_specs=pl.BlockSpec((tm, d), lambda i: (i, 0)),
        out_shape=jax.ShapeDtypeStruct((rows, d), f32),
        compiler_params=pltpu.CompilerParams(
            dimension_semantics=("arbitrary",), vmem_limit_bytes=VMEM_LIMIT),
        name="out_proj",
    )(x, mrg, w, g)


def _trunk(x, mem, wts):
    batch, seq, d = x.shape
    x2 = x.reshape(batch * seq, d)
    mem2 = mem.reshape(-1, d)
    p, gt = _norm_matmul(x2, wts["norm_g"], wts["w_main"], wts["w_gate_t"], wts["b_gate"], tm=1024, tn=D_MODEL,
                         w_transposed=True, col_act=IN_PROJ_ACT)
    kv, = _norm_matmul(mem2, wts["mem_g"], wts["w_kv"], tm=1024, tn=1024)
    nc = seq // CHUNK
    gt = gt.reshape(NH_M, GATE_ROWS, batch, nc, CHUNK)[:, :4].transpose(1, 2, 0, 3, 4)
    gs = _gate_scan(gt.reshape(4, batch * NH_M * nc, CHUNK), nc=nc)
    ym = _mlstm(p, gs, wts["mh_g"], batch=batch, seq=seq)
    mrg = _branches(p, ym, kv, wts["conv_w"], wts["w_branch"], batch=batch, seq=seq, tm=256)
    y = _out_proj(x2, mrg, wts["w_out"], wts["final_g"], tm=512)
    return y.reshape(batch, seq, d)


def _drop_gate_rows_kernel(a_ref, b_ref, o_ref, *, first_shifted):
    i = pl.program_id(0)

    @pl.when(i < first_shifted)
    def _():
        o_ref[...] = a_ref[...].astype(bf16)

    @pl.when(i >= first_shifted)
    def _():
        rb = a_ref.shape[0]
        o_ref[:rb - N_GATE, :] = a_ref[N_GATE:, :].astype(bf16)
        o_ref[rb - N_GATE:, :] = b_ref[...].astype(bf16)


def _drop_gate_rows(wt, g0, rb=1024):
    n, d = wt.shape
    n_out = n - N_GATE
    assert n_out % rb == 0 and g0 % rb == 0 and rb % N_GATE == 0 and N_GATE % 16 == 0
    return pl.pallas_call(
        functools.partial(_drop_gate_rows_kernel, first_shifted=g0 // rb),
        grid=(n_out // rb,),
        in_specs=[pl.BlockSpec((rb, d), lambda i: (i, 0)),
                  pl.BlockSpec((N_GATE, d), lambda i: ((i + 1) * (rb // N_GATE), 0))],
        out_specs=pl.BlockSpec((rb, d), lambda i: (i, 0)),
        out_shape=jax.ShapeDtypeStruct((n_out, d), bf16),
        compiler_params=pltpu.CompilerParams(dimension_semantics=("arbitrary",), vmem_limit_bytes=VMEM_LIMIT),
        name="drop_gate_rows",
    )(wt, wt)


def kernel(x_prompt, x_sample, mem_prompt, mem_sample, norm_g, w_in, b_if, conv_w, mem_norm_g, w_kv_mem,
           mh_norm_g, w_branch, w_out, final_norm_g):
    assert norm_g.shape[0] == 1, "single-layer trunk"
    wt = w_in[0].T
    g0 = 5 * D_MODEL
    w_main = _drop_gate_rows(wt, g0)
    wg = wt[g0:g0 + N_GATE].reshape(4, NH_M, D_MODEL).transpose(1, 0, 2)
    wg = jnp.pad(wg, ((0, 0), (0, GATE_ROWS - 4), (0, 0))).reshape(NH_M * GATE_ROWS, D_MODEL).astype(bf16)
    bg = jnp.pad(b_if[0].reshape(4, NH_M).T, ((0, 0), (0, GATE_ROWS - 4))).reshape(NH_M * GATE_ROWS, 1)
    wts = dict(
        norm_g=norm_g[0][None, :], w_main=w_main, w_gate_t=wg, b_gate=bg.astype(f32),
        mem_g=mem_norm_g[0][None, :], w_kv=w_kv_mem[0].astype(bf16), mh_g=mh_norm_g[0][None, :],
        conv_w=conv_w[0], w_branch=w_branch[0].astype(bf16), w_out=w_out[0].astype(bf16),
        final_g=final_norm_g[None, :])
    return (_trunk(x_prompt, mem_prompt, wts), _trunk(x_sample, mem_sample, wts))
```

```python
import functools
import math

import jax
import jax.numpy as jnp
from jax import lax
from jax.experimental import pallas as pl
from jax.experimental.pallas import tpu as pltpu

D_MODEL = 2048
NH_M = 8
HD_M = D_MODEL // NH_M
CHUNK = 128
NH_A = 4
HD_A = D_MODEL // NH_A
N_GATE = 4 * NH_M
GATE_ROWS = 8
EPS = 1e-6
COL_QM, COL_KM, COL_VM, COL_OM, COL_ZM, COL_CB, COL_CC, COL_CX, COL_ZC, COL_QA, COL_ZA, COL_MG = range(12)
N_MAIN = 14 * D_MODEL
HALO = 16
VMEM_LIMIT = 56 * 1024 * 1024

f32 = jnp.float32
bf16 = jnp.bfloat16


def _sigmoid(x):
    return 0.5 * jnp.tanh(0.5 * x) + 0.5


def _silu(x):
    half = 0.5 * x
    return half * jnp.tanh(half) + half


_ACTIVATIONS = {"id": lambda x: x, "sigmoid": _sigmoid, "silu": _silu}
IN_PROJ_ACT = ("id", "id", "id", "sigmoid", "silu",
               "id", "id", "id", "silu",
               "id", "silu", "sigmoid", "sigmoid", "sigmoid")


def _nt_dot(a, b):
    return lax.dot_general(a, b, (((1,), (1,)), ((), ())), preferred_element_type=f32)


def _norm_matmul_kernel(x_ref, g_ref, w_ref, *rest, with_gates, w_transposed, col_act):
    if with_gates:
        wgt_ref, bif_ref, p_ref, gt_ref, h_scr = rest
    else:
        p_ref, h_scr = rest

    @pl.when(pl.program_id(1) == 0)
    def _():
        xf = x_ref[...]
        ms = jnp.mean(xf * xf, axis=-1, keepdims=True)
        h = (xf * lax.rsqrt(ms + EPS) * g_ref[...]).astype(bf16)
        h_scr[...] = h
        if with_gates:
            gt_ref[...] = _nt_dot(wgt_ref[...], h) + bif_ref[...]

    dot = _nt_dot if w_transposed else functools.partial(jnp.dot, preferred_element_type=f32)

    def emit(act):
        p_ref[...] = act(dot(h_scr[...], w_ref[...])).astype(p_ref.dtype)

    if col_act is None:
        emit(_ACTIVATIONS["id"])
    else:
        j = pl.program_id(1)
        for name in sorted(set(col_act)):
            hit = functools.reduce(jnp.logical_or, [j == jj for jj, a in enumerate(col_act) if a == name])
            pl.when(hit)(functools.partial(emit, _ACTIVATIONS[name]))


def _norm_matmul(x, g, w, wgt=None, bif=None, *, tm, tn, w_transposed=False, col_act=None):
    rows, d = x.shape
    n = w.shape[0] if w_transposed else w.shape[1]
    tm = min(tm, rows)
    tn = min(tn, n)
    assert rows % tm == 0 and n % tn == 0, (rows, n, tm, tn)
    assert col_act is None or len(col_act) == n // tn
    with_gates = wgt is not None
    in_specs = [
        pl.BlockSpec((tm, d), lambda i, j: (i, 0)),
        pl.BlockSpec((1, d), lambda i, j: (0, 0)),
        pl.BlockSpec((tn, d), lambda i, j: (j, 0)) if w_transposed else pl.BlockSpec((d, tn), lambda i, j: (0, j)),
    ]
    out_shape = [jax.ShapeDtypeStruct((rows, n), bf16)]
    out_specs = [pl.BlockSpec((tm, tn), lambda i, j: (i, j))]
    args = [x, g, w]
    if with_gates:
        ng = wgt.shape[0]
        in_specs += [pl.BlockSpec((ng, d), lambda i, j: (0, 0)), pl.BlockSpec((ng, 1), lambda i, j: (0, 0))]
        out_shape.append(jax.ShapeDtypeStruct((ng, rows), f32))
        out_specs.append(pl.BlockSpec((ng, tm), lambda i, j: (0, i)))
        args += [wgt, bif]
    return pl.pallas_call(
        functools.partial(_norm_matmul_kernel, with_gates=with_gates, w_transposed=w_transposed, col_act=col_act),
        grid=(rows // tm, n // tn),
        in_specs=in_specs,
        out_specs=out_specs,
        out_shape=out_shape,
        scratch_shapes=[pltpu.VMEM((tm, d), bf16)],
        compiler_params=pltpu.CompilerParams(
            dimension_semantics=("arbitrary", "arbitrary"), vmem_limit_bytes=VMEM_LIMIT),
        name="norm_matmul_gates" if with_gates else "norm_matmul",
    )(*args)


def _lane_scan(x, op, reverse):
    lane = lax.broadcasted_iota(jnp.int32, x.shape, 1)
    n = x.shape[1]
    sh = 1
    while sh < n:
        if reverse:
            x = jnp.where(lane < n - sh, op(x, pltpu.roll(x, n - sh, axis=1)), x)
        else:
            x = jnp.where(lane >= sh, op(x, pltpu.roll(x, sh, axis=1)), x)
        sh *= 2
    return x


def _col_bcast(row):
    L = row.shape[1]
    return jnp.broadcast_to(row, (L, L)).T


GS_W, GS_G, GS_A, GS_ENM, GS_WK, GS_DECAY = range(6)


def _gate_scan_kernel(gt_ref, out_ref, *, nc):
    _, R, L = gt_ref.shape
    log_k_scale = -0.5 * math.log(HD_M)
    ck = lax.broadcasted_iota(jnp.int32, (R, L), 0) % nc
    for dr in range(2):
        rev = dr == 1
        i_pre, f_pre = gt_ref[dr], gt_ref[2 + dr]
        lf = jnp.minimum(f_pre, 0.0) - jnp.log1p(jnp.exp(-jnp.abs(f_pre)))
        b = _lane_scan(lf, jnp.add, rev)
        tot = b + _lane_scan(lf, jnp.add, not rev) - lf
        w = i_pre - b
        cw = _lane_scan(w, jnp.maximum, rev)
        mw = jnp.maximum(cw, _lane_scan(w, jnp.maximum, not rev))

        a, c = tot, mw + tot
        sh = 1
        while sh < nc:
            shift, valid = (R - sh, ck < nc - sh) if rev else (sh, ck >= sh)
            c = jnp.where(valid, jnp.maximum(pltpu.roll(c, shift, axis=0) + a, c), c)
            a = jnp.where(valid, pltpu.roll(a, shift, axis=0) + a, a)
            sh *= 2
        m_after = jnp.maximum(a, c)
        shift, valid = (R - 1, ck < nc - 1) if rev else (1, ck >= 1)
        m = jnp.where(valid, pltpu.roll(m_after, shift, axis=0), 0.0)

        g = jnp.maximum(cw, m)
        m_up = jnp.maximum(m, mw)
        w = w + log_k_scale
        out_ref[dr, GS_W] = w
        out_ref[dr, GS_G] = g
        out_ref[dr, GS_A] = jnp.exp(m - g)
        out_ref[dr, GS_ENM] = jnp.exp(-(g + b))
        out_ref[dr, GS_WK] = jnp.exp(w - m_up)
        out_ref[dr, GS_DECAY] = jnp.exp(m - m_up)


def _gate_scan(gt, *, nc):
    _, rows, L = gt.shape
    rb = min(rows, 256)
    assert rows % rb == 0 and rb % nc == 0 and rb % 8 == 0, (rows, nc)
    return pl.pallas_call(
        functools.partial(_gate_scan_kernel, nc=nc),
        grid=(rows // rb,),
        in_specs=[pl.BlockSpec((4, rb, L), lambda i: (0, i, 0))],
        out_specs=pl.BlockSpec((2, 6, rb, L), lambda i: (0, 0, i, 0)),
        out_shape=jax.ShapeDtypeStruct((2, 6, rows, L), f32),
        compiler_params=pltpu.CompilerParams(dimension_semantics=("arbitrary",)),
        name="gate_scan",
    )(gt)


def _mlstm_kernel(q_ref, k_ref, v_ref, o_ref, z_ref, gs_ref, mhg_ref, y_ref, h_scr):
    S, d = q_ref.shape
    L = CHUNK
    nc = S // L

    r = lax.broadcasted_iota(jnp.int32, (L, L), 0)
    c = lax.broadcasted_iota(jnp.int32, (L, L), 1)

    def prepare(ci, dr):
        causal = (c >= r) if dr == 1 else (c <= r)
        rows = pl.ds(ci * L, L)
        one = pl.ds(ci, 1)
        qc, kc, vc = q_ref[rows, :], k_ref[rows, :], v_ref[rows, :]
        g = _col_bcast(gs_ref[dr, GS_G, one, :])
        a = _col_bcast(gs_ref[dr, GS_A, one, :])
        exp_neg_mt = _col_bcast(gs_ref[dr, GS_ENM, one, :])
        e = jnp.exp(jnp.where(causal, gs_ref[dr, GS_W, one, :] - g, -jnp.inf))
        wk = gs_ref[dr, GS_WK, one, :]
        kw = kc.T * wk.astype(bf16)
        decay = gs_ref[dr, GS_DECAY, one, :]
        decay = jnp.concatenate([decay, decay], axis=1)
        return rows, qc, kc, vc, a, exp_neg_mt, e, wk, kw, decay

    def advance(dr, prepared, cx, n8):
        rows, qc, kc, vc, a, exp_neg_mt, e, wk, kw, decay = prepared
        kn = jnp.concatenate([kc, jnp.broadcast_to(n8[0:1, :].astype(bf16), (L, d))], axis=0)
        qkn = _nt_dot(qc, kn)
        s = qkn[:, :L] * e
        hx = (jnp.dot(s.astype(bf16), vc, preferred_element_type=f32)
              + jnp.concatenate([a, a], axis=1) * jnp.dot(qc, cx.astype(bf16), preferred_element_type=f32))
        den = jnp.maximum(jnp.abs(jnp.sum(s, axis=1, keepdims=True) + a * qkn[:, L:]), exp_neg_mt)
        rden = 1.0 / den
        h_scr[dr, rows, :] = hx * jnp.concatenate([rden, rden], axis=1)
        return (decay * cx + jnp.dot(kw, vc, preferred_element_type=f32),
                decay * n8 + jnp.dot(jnp.broadcast_to(wk, (8, L)).astype(bf16), kc, preferred_element_type=f32))

    def finish(ci):
        rows = pl.ds(ci * L, L)
        h = h_scr[0, rows, :] + h_scr[1, rows, :]
        mu = jnp.mean(h, axis=-1, keepdims=True)
        hc = h - mu
        var = jnp.mean(hc * hc, axis=-1, keepdims=True)
        hn = hc * lax.rsqrt(var + EPS) * mhg_ref[...]
        y_ref[rows, :] = hn.astype(bf16) * (o_ref[rows, :] * z_ref[rows, :])

    ready = []
    state = [(jnp.zeros((d, d), f32), jnp.zeros((8, d), f32))] * 2
    prepared = [prepare(0, 0), prepare(nc - 1, 1)]
    for it in range(nc):
        following = [prepare(it + 1, 0), prepare(nc - 2 - it, 1)] if it + 1 < nc else None
        for dr in range(2):
            state[dr] = advance(dr, prepared[dr], *state[dr])
            if dr < len(ready):
                finish(ready[dr])
        prepared = following
        ready = sorted({it, nc - 1 - it}) if it >= nc // 2 else []
    for ci in ready:
        finish(ci)


def _mlstm(p, gs, mhg, *, batch, seq):
    nc = seq // CHUNK
    assert seq % CHUNK == 0 and nc % 8 == 0, seq

    def col(cb):
        return pl.BlockSpec((seq, HD_M), lambda b, h, cb=cb: (b, cb * NH_M + h))

    return pl.pallas_call(
        _mlstm_kernel,
        grid=(batch, NH_M),
        in_specs=[col(COL_QM), col(COL_KM), col(COL_VM), col(COL_OM), col(COL_ZM),
                  pl.BlockSpec((2, 6, nc, CHUNK), lambda b, h: (0, 0, b * NH_M + h, 0)),
                  pl.BlockSpec((1, HD_M), lambda b, h: (0, h))],
        out_specs=pl.BlockSpec((seq, HD_M), lambda b, h: (b, h)),
        out_shape=jax.ShapeDtypeStruct((batch * seq, D_MODEL), bf16),
        scratch_shapes=[pltpu.VMEM((2, seq, HD_M), f32)],
        compiler_params=pltpu.CompilerParams(
            dimension_semantics=("arbitrary", "arbitrary"), vmem_limit_bytes=VMEM_LIMIT),
        name="mlstm",
    )(p, p, p, p, p, gs, mhg)


def _branches_kernel(cb_ref, cc_ref, cx_ref, zc_ref, ccp_ref, cxp_ref, ccn_ref, cxn_ref,
                     qa_ref, za_ref, mg0_ref, mg1_ref, mg2_ref, ym_ref, km_ref, vm_ref,
                     convw_ref, wb_ref, out_ref):
    tm = cb_ref.shape[0]
    i = pl.program_id(1)
    ni = pl.num_programs(1)

    slabs = [slice(c0, c0 + HD_A) for c0 in range(0, D_MODEL, HD_A)]

    def gated_proj(y, k, gate_ref, cols):
        return gate_ref[:, cols] * jnp.dot(y, wb_ref[k, :, cols], preferred_element_type=f32).astype(bf16)

    row = lax.broadcasted_iota(jnp.int32, (tm, 1), 0)
    first = (i > 0).astype(f32)
    last = (i < ni - 1).astype(f32)

    def conv_branch(cols):
        u = (cc_ref[:, cols] * cx_ref[:, cols]).astype(f32)
        u_before = (ccp_ref[HALO - 1:HALO, cols] * cxp_ref[HALO - 1:HALO, cols]).astype(f32) * first
        u_after = (ccn_ref[0:1, cols] * cxn_ref[0:1, cols]).astype(f32) * last
        u_prev = jnp.where(row == 0, u_before, pltpu.roll(u, 1, axis=0))
        u_next = jnp.where(row == tm - 1, u_after, pltpu.roll(u, tm - 1, axis=0))
        w = convw_ref[:, cols]
        conv = w[0:1, :] * u_prev + w[1:2, :] * u + w[2:3, :] * u_next
        return conv.astype(bf16) * (cb_ref[:, cols] * zc_ref[:, cols])

    def attention_branch(cols):
        s = _nt_dot(qa_ref[:, cols], km_ref[:, cols]) * (HD_A ** -0.5)
        s = s - jnp.max(s, axis=-1, keepdims=True)
        e = jnp.exp(s)
        p = e / jnp.sum(e, axis=-1, keepdims=True)
        o = jnp.dot(p.astype(bf16), vm_ref[:, cols], preferred_element_type=f32)
        return o.astype(bf16) * za_ref[:, cols]

    ym = ym_ref[...]
    merged, y_c, y_a = [], [], []
    for cols in slabs:
        merged.append(gated_proj(ym, 0, mg0_ref, cols))
        y_c.append(conv_branch(cols))
    y_c = jnp.concatenate(y_c, axis=1)
    for n, cols in enumerate(slabs):
        merged[n] = merged[n] + gated_proj(y_c, 1, mg1_ref, cols)
        y_a.append(attention_branch(cols))
    y_a = jnp.concatenate(y_a, axis=1)
    for n, cols in enumerate(slabs):
        out_ref[:, cols] = merged[n] + gated_proj(y_a, 2, mg2_ref, cols)


def _branches(p, ym, kv, convw, wb, *, batch, seq, tm):
    tm = min(tm, seq)
    assert seq % tm == 0 and tm % HALO == 0, (seq, tm)
    nb = seq // tm
    hb = tm // HALO
    last_halo = batch * seq // HALO - 1

    def col(cb):
        return pl.BlockSpec((tm, D_MODEL), lambda b, i, cb=cb: (b * nb + i, cb))

    def halo_before(cb):
        return pl.BlockSpec((HALO, D_MODEL), lambda b, i, cb=cb: (jnp.maximum((b * nb + i) * hb - 1, 0), cb))

    def halo_after(cb):
        return pl.BlockSpec((HALO, D_MODEL),
                            lambda b, i, cb=cb: (jnp.minimum((b * nb + i + 1) * hb, last_halo), cb))

    n_mem = kv.shape[0] // batch
    const = lambda *shape: pl.BlockSpec(shape, lambda b, i: (0,) * len(shape), pipeline_mode=pl.Buffered(1))
    return pl.pallas_call(
        _branches_kernel,
        grid=(batch, nb),
        in_specs=[col(COL_CB), col(COL_CC), col(COL_CX), col(COL_ZC),
                  halo_before(COL_CC), halo_before(COL_CX), halo_after(COL_CC), halo_after(COL_CX),
                  col(COL_QA), col(COL_ZA), col(COL_MG), col(COL_MG + 1), col(COL_MG + 2),
                  pl.BlockSpec((tm, D_MODEL), lambda b, i: (b * nb + i, 0)),
                  pl.BlockSpec((n_mem, D_MODEL), lambda b, i: (b, 0)),
                  pl.BlockSpec((n_mem, D_MODEL), lambda b, i: (b, 1)),
                  const(3, D_MODEL), const(3, D_MODEL, D_MODEL)],
        out_specs=pl.BlockSpec((tm, D_MODEL), lambda b, i: (b * nb + i, 0)),
        out_shape=jax.ShapeDtypeStruct((batch * seq, D_MODEL), bf16),
        compiler_params=pltpu.CompilerParams(
            dimension_semantics=("arbitrary", "arbitrary"), vmem_limit_bytes=VMEM_LIMIT),
        name="branches",
    )(p, p, p, p, p, p, p, p, p, p, p, p, p, ym, kv, kv, convw, wb)


def _out_proj_kernel(x_ref, mrg_ref, w_ref, g_ref, y_ref):
    r = x_ref[...] + jnp.dot(mrg_ref[...], w_ref[...], preferred_element_type=f32)
    ms = jnp.mean(r * r, axis=-1, keepdims=True)
    y_ref[...] = r * lax.rsqrt(ms + EPS) * g_ref[...]


def _out_proj(x, mrg, w, g, *, tm):
    rows, d = x.shape
    tm = min(tm, rows)
    assert rows % tm == 0, (rows, tm)
    return pl.pallas_call(
        _out_proj_kernel,
        grid=(rows // tm,),
        in_specs=[pl.BlockSpec((tm, d), lambda i: (i, 0)),
                  pl.BlockSpec((tm, d), lambda i: (i, 0)),
                  pl.BlockSpec((d, d), lambda i: (0, 0)),
                  pl.BlockSpec((1, d), lambda i: (0, 0))],
        out_specs=pl.BlockSpec((tm, d), lambda i: (i, 0)),
        out_shape=jax.ShapeDtypeStruct((rows, d), f32),
        compiler_params=pltpu.CompilerParams(
            dimension_semantics=("arbitrary",), vmem_limit_bytes=VMEM_LIMIT),
        name="out_proj",
    )(x, mrg, w, g)


def _trunk(x, mem, wts):
    batch, seq, d = x.shape
    x2 = x.reshape(batch * seq, d)
    mem2 = mem.reshape(-1, d)
    p, gt = _norm_matmul(x2, wts["norm_g"], wts["w_main"], wts["w_gate_t"], wts["b_gate"], tm=1024, tn=D_MODEL,
                         w_transposed=True, col_act=IN_PROJ_ACT)
    kv, = _norm_matmul(mem2, wts["mem_g"], wts["w_kv"], tm=1024, tn=1024)
    nc = seq // CHUNK
    gt = gt.reshape(NH_M, GATE_ROWS, batch, nc, CHUNK)[:, :4].transpose(1, 2, 0, 3, 4)
    gs = _gate_scan(gt.reshape(4, batch * NH_M * nc, CHUNK), nc=nc)
    ym = _mlstm(p, gs, wts["mh_g"], batch=batch, seq=seq)
    mrg = _branches(p, ym, kv, wts["conv_w"], wts["w_branch"], batch=batch, seq=seq, tm=256)
    y = _out_proj(x2, mrg, wts["w_out"], wts["final_g"], tm=512)
    return y.reshape(batch, seq, d)


def _drop_gate_rows_kernel(a_ref, b_ref, o_ref, *, first_shifted):
    i = pl.program_id(0)

    @pl.when(i < first_shifted)
    def _():
        o_ref[...] = a_ref[...].astype(bf16)

    @pl.when(i >= first_shifted)
    def _():
        rb = a_ref.shape[0]
        o_ref[:rb - N_GATE, :] = a_ref[N_GATE:, :].astype(bf16)
        o_ref[rb - N_GATE:, :] = b_ref[...].astype(bf16)


def _drop_gate_rows(wt, g0, rb=1024):
    n, d = wt.shape
    n_out = n - N_GATE
    assert n_out % rb == 0 and g0 % rb == 0 and rb % N_GATE == 0 and N_GATE % 16 == 0
    return pl.pallas_call(
        functools.partial(_drop_gate_rows_kernel, first_shifted=g0 // rb),
        grid=(n_out // rb,),
        in_specs=[pl.BlockSpec((rb, d), lambda i: (i, 0)),
                  pl.BlockSpec((N_GATE, d), lambda i: ((i + 1) * (rb // N_GATE), 0))],
        out_specs=pl.BlockSpec((rb, d), lambda i: (i, 0)),
        out_shape=jax.ShapeDtypeStruct((n_out, d), bf16),
        compiler_params=pltpu.CompilerParams(dimension_semantics=("arbitrary",), vmem_limit_bytes=VMEM_LIMIT),
        name="drop_gate_rows",
    )(wt, wt)


def kernel(x_prompt, x_sample, mem_prompt, mem_sample, norm_g, w_in, b_if, conv_w, mem_norm_g, w_kv_mem,
           mh_norm_g, w_branch, w_out, final_norm_g):
    assert norm_g.shape[0] == 1, "single-layer trunk"
    wt = w_in[0].T
    g0 = 5 * D_MODEL
    w_main = _drop_gate_rows(wt, g0)
    wg = wt[g0:g0 + N_GATE].reshape(4, NH_M, D_MODEL).transpose(1, 0, 2)
    wg = jnp.pad(wg, ((0, 0), (0, GATE_ROWS - 4), (0, 0))).reshape(NH_M * GATE_ROWS, D_MODEL).astype(bf16)
    bg = jnp.pad(b_if[0].reshape(4, NH_M).T, ((0, 0), (0, GATE_ROWS - 4))).reshape(NH_M * GATE_ROWS, 1)
    wts = dict(
        norm_g=norm_g[0][None, :], w_main=w_main, w_gate_t=wg, b_gate=bg.astype(f32),
        mem_g=mem_norm_g[0][None, :], w_kv=w_kv_mem[0].astype(bf16), mh_g=mh_norm_g[0][None, :],
        conv_w=conv_w[0], w_branch=w_branch[0].astype(bf16), w_out=w_out[0].astype(bf16),
        final_g=final_norm_g[None, :])
    return (_trunk(x_prompt, mem_prompt, wts), _trunk(x_sample, mem_sample, wts))
```
